```python
import math
import jax, jax.numpy as jnp
from jax import lax
import numpy as np

D_MODEL = 2048
BATCH = 4
SEQ = 4096
DEPTH = 1

HEAD_DIM = 128
ATTN_WIDTH = 3 * D_MODEL // 4
N_ATTN_HEADS = ATTN_WIDTH // HEAD_DIM
GMLP_WIDTH = D_MODEL // 4
GMLP_GROUP_WIDTH = 128
GMLP_GROUPS = GMLP_WIDTH // GMLP_GROUP_WIDTH
GMLP_CHUNK = 128
MIX_WIDTH = ATTN_WIDTH + GMLP_WIDTH
IN_PROJ_WIDTH = 3 * ATTN_WIDTH + 2 * GMLP_WIDTH
DILATED_PAIRS = ((128, 1), (512, 4), (2048, 16))
ATTN_BLOCK = 128
ROPE_THETA = 10000.0

N_EXPERTS = 32
TOP_K = 4
D_FF = D_MODEL
SWIGLU_LIMIT = 7.0
SWIGLU_ALPHA = 1.702
MOE_BLOCK = 128

DEEPNORM_ALPHA = (2 * DEPTH) ** 0.25
DEEPNORM_BETA = (8 * DEPTH) ** -0.25
LN_EPS = 1e-5
NEG_INF = -1e30

kernel_name = "hybrid_dilated_attn_gmlp_moe_deepnorm_adaln"


def layer_norm(x, g=None, b=None):
    xf = x.astype(jnp.float32)
    mu = jnp.mean(xf, axis=-1, keepdims=True)
    xc = xf - mu
    y = xc * lax.rsqrt(jnp.mean(xc * xc, axis=-1, keepdims=True) + LN_EPS)
    if g is not None:
        y = y * g.astype(jnp.float32) + b.astype(jnp.float32)
    return y.astype(x.dtype)


def rope(x, positions):
    half = HEAD_DIM // 2
    inv_freq = jnp.power(ROPE_THETA, -jnp.arange(half, dtype=jnp.float32) * (2.0 / HEAD_DIM))
    ang = positions.astype(jnp.float32)[:, :, None] * inv_freq
    cos = jnp.cos(ang)[:, :, None, :]
    sin = jnp.sin(ang)[:, :, None, :]
    xf = x.astype(jnp.float32)
    x1, x2 = xf[..., :half], xf[..., half:]
    return jnp.concatenate([x1 * cos - x2 * sin, x2 * cos + x1 * sin], axis=-1).astype(x.dtype)


def dilated_branch(q, k, v, window, dil):
    B, H, S, hd = q.shape
    max_rel = window // dil
    assert max_rel <= ATTN_BLOCK
    L = S // dil
    nb = -(-L // ATTN_BLOCK)
    pad = nb * ATTN_BLOCK - L

    def strided(a):
        return a.reshape(B, H, L, dil, hd).transpose(0, 1, 3, 2, 4)

    qs, ks, vs = strided(q), strided(k), strided(v)
    qs = jnp.pad(qs, ((0, 0), (0, 0), (0, 0), (0, pad), (0, 0)))
    kpad = ((0, 0), (0, 0), (0, 0), (ATTN_BLOCK, pad), (0, 0))
    ks = jnp.pad(ks, kpad).reshape(B, H, dil, nb + 1, ATTN_BLOCK, hd)
    vs = jnp.pad(vs, kpad).reshape(B, H, dil, nb + 1, ATTN_BLOCK, hd)
    qb = qs.reshape(B, H, dil, nb, ATTN_BLOCK, hd)
    kw = jnp.concatenate([ks[:, :, :, :-1], ks[:, :, :, 1:]], axis=-2)
    vw = jnp.concatenate([vs[:, :, :, :-1], vs[:, :, :, 1:]], axis=-2)

    logits = jnp.einsum('bhrnqd,bhrnkd->bhrnqk', qb, kw,
                        preferred_element_type=jnp.float32) * (1.0 / math.sqrt(HEAD_DIM))
    qi = jnp.arange(ATTN_BLOCK)[:, None]
    ki = jnp.arange(2 * ATTN_BLOCK)[None, :]
    dist = qi - ki + ATTN_BLOCK
    rel_ok = (dist >= 0) & (dist <= max_rel)
    key_pos = jnp.arange(nb)[:, None, None] * ATTN_BLOCK + ki[None] - ATTN_BLOCK
    mask = rel_ok[None] & (key_pos >= 0)
    logits = jnp.where(mask, logits, NEG_INF)
    lse = jax.nn.logsumexp(logits, axis=-1)
    p = jnp.exp(logits - lse[..., None])
    o = jnp.einsum('bhrnqk,bhrnkd->bhrnqd', p.astype(v.dtype), vw,
                   preferred_element_type=jnp.float32)
    o = o.reshape(B, H, dil, nb * ATTN_BLOCK, hd)[:, :, :, :L]
    o = o.transpose(0, 1, 3, 2, 4).reshape(B, H, S, hd)
    lse = lse.reshape(B, H, dil, nb * ATTN_BLOCK)[:, :, :, :L].transpose(0, 1, 3, 2).reshape(B, H, S)
    return o, lse


def hybrid_mixer(h, positions, w_in, w_spatial, b_spatial, gmlp_ln_g, gmlp_ln_b, w_out):
    B, S, _ = h.shape
    proj = h @ w_in
    A, G = ATTN_WIDTH, GMLP_WIDTH
    q, k, v, u, vg = jnp.split(proj, [A, 2 * A, 3 * A, 3 * A + G], axis=-1)

    q = rope(q.reshape(B, S, N_ATTN_HEADS, HEAD_DIM), positions).transpose(0, 2, 1, 3)
    k = rope(k.reshape(B, S, N_ATTN_HEADS, HEAD_DIM), positions).transpose(0, 2, 1, 3)
    v = v.reshape(B, S, N_ATTN_HEADS, HEAD_DIM).transpose(0, 2, 1, 3)
    outs, lses = zip(*[dilated_branch(q, k, v, w, d) for (w, d) in DILATED_PAIRS])
    mix_w = jax.nn.softmax(jnp.stack(lses), axis=0)
    attn = jnp.einsum('rbhs,rbhsd->bshd', mix_w, jnp.stack(outs)).reshape(B, S, A).astype(h.dtype)

    u = jax.nn.gelu(u, approximate=False)
    vg = layer_norm(jax.nn.gelu(vg, approximate=False), gmlp_ln_g, gmlp_ln_b)
    nc = S // GMLP_CHUNK
    vg = vg.reshape(B, nc, GMLP_CHUNK, GMLP_GROUPS, GMLP_GROUP_WIDTH)
    causal = jnp.tril(jnp.ones((GMLP_CHUNK, GMLP_CHUNK), dtype=bool))
    ws = jnp.where(causal[None], w_spatial, 0.0).astype(vg.dtype)
    spatial = jnp.einsum('gts,bnsgc->bntgc', ws, vg) + b_spatial.T[None, None, :, :, None]
    gm = (u.reshape(B, nc, GMLP_CHUNK, GMLP_GROUPS, GMLP_GROUP_WIDTH) * spatial).reshape(B, S, G)

    return jnp.concatenate([attn, gm.astype(h.dtype)], axis=-1) @ w_out


def moe_ffn(h, router_w, router_b, w_gate_up, b_gate_up, w_down, b_down):
    B, S, D = h.shape
    T = B * S
    xs = h.reshape(T, D)
    logits = (xs @ router_w + router_b).astype(jnp.float32)
    top_logits, top_idx = lax.top_k(logits, TOP_K)
    gates = jax.nn.softmax(top_logits, axis=-1)

    n_assign = T * TOP_K
    expert_of = top_idx.reshape(n_assign)
    token_of = jnp.repeat(jnp.arange(T, dtype=jnp.int32), TOP_K)
    gate_of = gates.reshape(n_assign)
    order = jnp.argsort(expert_of)
    e_sorted = expert_of[order]
    counts = jnp.bincount(expert_of, length=N_EXPERTS)
    starts = jnp.cumsum(counts) - counts
    padded = (counts + MOE_BLOCK - 1) // MOE_BLOCK * MOE_BLOCK
    pad_end = jnp.cumsum(padded)
    pad_start = pad_end - padded
    slot = pad_start[e_sorted] + jnp.arange(n_assign) - starts[e_sorted]
    n_slots = (-(-n_assign // MOE_BLOCK) + N_EXPERTS) * MOE_BLOCK
    n_blocks = n_slots // MOE_BLOCK
    slot_token = jnp.zeros((n_slots,), jnp.int32).at[slot].set(token_of[order])
    slot_gate = jnp.zeros((n_slots,), jnp.float32).at[slot].set(gate_of[order])
    block_expert = jnp.minimum(
        jnp.searchsorted(pad_end, jnp.arange(n_blocks) * MOE_BLOCK, side='right'), N_EXPERTS - 1)
    x_blocks = xs[slot_token].reshape(n_blocks, MOE_BLOCK, D)

    def expert_block(args):
        xb, e = args
        gu = (xb @ w_gate_up[e] + b_gate_up[e]).astype(jnp.float32)
        gate, up = gu[:, :D_FF], gu[:, D_FF:]
        gate = jnp.minimum(gate, SWIGLU_LIMIT)
        up = jnp.clip(up, -SWIGLU_LIMIT, SWIGLU_LIMIT)
        act = ((up + 1.0) * (gate * jax.nn.sigmoid(gate * SWIGLU_ALPHA))).astype(xb.dtype)
        return act @ w_down[e] + b_down[e]

    y_blocks = lax.map(expert_block, (x_blocks, block_expert))
    y = jnp.zeros((T, D), jnp.float32).at[slot_token].add(
        y_blocks.reshape(n_slots, D).astype(jnp.float32) * slot_gate[:, None])
    return y.reshape(B, S, D).astype(h.dtype)


def setup_inputs(seed: int = 0) -> dict:
    key = jax.random.key(seed)
    ks = jax.random.split(key, 22)
    f32 = jnp.float32
    nrm = lambda k, shape, s: jax.random.normal(k, shape, f32) * s
    L = DEPTH
    x = jax.random.normal(ks[0], (BATCH, SEQ, D_MODEL), f32)
    c = jax.random.normal(ks[1], (BATCH, D_MODEL), f32)
    positions = (jnp.arange(SEQ, dtype=jnp.int32)[None, :]
                 + jax.random.randint(ks[2], (BATCH, 1), 0, 1024, dtype=jnp.int32))
    return {
        "x": x,
        "c": c,
        "positions": positions,
        "ada_w": nrm(ks[3], (L, D_MODEL, 6 * D_MODEL), D_MODEL ** -0.5),
        "ada_b": nrm(ks[4], (L, 6 * D_MODEL), 0.02),
        "w_in": nrm(ks[5], (L, D_MODEL, IN_PROJ_WIDTH), D_MODEL ** -0.5),
        "w_spatial": nrm(ks[6], (L, GMLP_GROUPS, GMLP_CHUNK, GMLP_CHUNK), GMLP_CHUNK ** -0.5),
        "b_spatial": 1.0 + nrm(ks[7], (L, GMLP_GROUPS, GMLP_CHUNK), 0.1),
        "gmlp_ln_g": 1.0 + nrm(ks[8], (L, GMLP_WIDTH), 0.05),
        "gmlp_ln_b": nrm(ks[9], (L, GMLP_WIDTH), 0.02),
        "w_out": nrm(ks[10], (L, MIX_WIDTH, D_MODEL), DEEPNORM_BETA * MIX_WIDTH ** -0.5),
        "ln1_g": 1.0 + nrm(ks[11], (L, D_MODEL), 0.05),
        "ln1_b": nrm(ks[12], (L, D_MODEL), 0.02),
        "router_w": nrm(ks[13], (L, D_MODEL, N_EXPERTS), D_MODEL ** -0.5),
        "router_b": nrm(ks[14], (L, N_EXPERTS), 0.01),
        "w_gate_up": nrm(ks[15], (L, N_EXPERTS, D_MODEL, 2 * D_FF), D_MODEL ** -0.5),
        "b_gate_up": nrm(ks[16], (L, N_EXPERTS, 2 * D_FF), 0.01),
        "w_down": nrm(ks[17], (L, N_EXPERTS, D_FF, D_MODEL), DEEPNORM_BETA * D_FF ** -0.5),
        "b_down": nrm(ks[18], (L, N_EXPERTS, D_MODEL), 0.01),
        "ln2_g": 1.0 + nrm(ks[19], (L, D_MODEL), 0.05),
        "ln2_b": nrm(ks[20], (L, D_MODEL), 0.02),
    }


def reference(x, c, positions, ada_w, ada_b, w_in, w_spatial, b_spatial, gmlp_ln_g, gmlp_ln_b,
              w_out, ln1_g, ln1_b, router_w, router_b, w_gate_up, b_gate_up, w_down, b_down,
              ln2_g, ln2_b):
    for l in range(DEPTH):
        mod = (jax.nn.silu(c) @ ada_w[l] + ada_b[l])[:, None, :]
        sh1, sc1, g1, sh2, sc2, g2 = jnp.split(mod, 6, axis=-1)

        h = layer_norm(x) * (1.0 + sc1) + sh1
        mix = hybrid_mixer(h, positions, w_in[l], w_spatial[l], b_spatial[l],
                           gmlp_ln_g[l], gmlp_ln_b[l], w_out[l])
        x = layer_norm(DEEPNORM_ALPHA * x + g1 * mix, ln1_g[l], ln1_b[l])

        h = layer_norm(x) * (1.0 + sc2) + sh2
        ffn = moe_ffn(h, router_w[l], router_b[l], w_gate_up[l], b_gate_up[l], w_down[l], b_down[l])
        x = layer_norm(DEEPNORM_ALPHA * x + g2 * ffn, ln2_g[l], ln2_b[l])
    return x
```

```python
import functools
import math

import jax
import jax.numpy as jnp
from jax import lax
from jax.experimental import pallas as pl
from jax.experimental.pallas import tpu as pltpu

HEAD_DIM = 128
ATTN_FRACTION_NUM, ATTN_FRACTION_DEN = 3, 4
GMLP_GROUP_WIDTH = 128
GMLP_CHUNK = 128
DILATIONS = (1, 4, 16)
ATTN_BLOCK = 128
ROPE_THETA = 10000.0
N_EXPERTS = 32
TOP_K = 4
SWIGLU_LIMIT = 7.0
SWIGLU_ALPHA = 1.702
DEPTH = 1
DEEPNORM_ALPHA = (2 * DEPTH) ** 0.25
LN_EPS = 1e-5
NEG_INF = -1e30
LANES = 128
SUBLANES = 8

VMEM_LIMIT = 56 * 1024 * 1024

F32 = jnp.float32
BF16 = jnp.bfloat16


def _ln(x):
    mu = jnp.mean(x, axis=-1, keepdims=True)
    xc = x - mu
    return xc * lax.rsqrt(jnp.mean(xc * xc, axis=-1, keepdims=True) + LN_EPS)


def _gelu(x):
    return 0.5 * x * (1.0 + lax.erf(x * (1.0 / math.sqrt(2.0))))


def _cparams(sem):
    return pltpu.CompilerParams(dimension_semantics=sem, vmem_limit_bytes=VMEM_LIMIT)


def _ada_body(c_ref, w_ref, b_ref, o_ref):
    c = c_ref[...]
    s = c * jax.nn.sigmoid(c)
    o_ref[...] = jnp.dot(s, w_ref[...], precision=lax.Precision.HIGHEST,
                         preferred_element_type=F32) + b_ref[...]


def _ada_call(c_pad, ada_w, ada_b):
    rows, d = c_pad.shape
    n = ada_w.shape[1]
    tn = 1024
    return pl.pallas_call(
        _ada_body,
        grid=(n // tn,),
        in_specs=[pl.BlockSpec((rows, d), lambda j: (0, 0)),
                  pl.BlockSpec((d, tn), lambda j: (0, j)),
                  pl.BlockSpec((1, tn), lambda j: (0, j))],
        out_specs=pl.BlockSpec((rows, tn), lambda j: (0, j)),
        out_shape=jax.ShapeDtypeStruct((rows, n), F32),
        compiler_params=_cparams(("arbitrary",)),
        name="ada",
    )(c_pad, ada_w, ada_b)


def _inproj_body(x_ref, pos_ref, mod_ref, invf_ref, w_ref, oq_ref, ou_ref,
                 hn_ref, cos_ref, sin_ref, *, n_q_tiles, n_rope_tiles, n_qkv_tiles, tn):
    j = pl.program_id(1)

    @pl.when(j == 0)
    def _():
        y = _ln(x_ref[...])
        sh = mod_ref[0:1, :]
        sc = mod_ref[1:2, :]
        hn_ref[...] = (y * (1.0 + sc) + sh).astype(BF16)
        ang = pos_ref[...].astype(F32) * invf_ref[...]
        lane = lax.broadcasted_iota(jnp.int32, (1, HEAD_DIM), 1)
        sign = jnp.where(lane < HEAD_DIM // 2, -1.0, 1.0).astype(F32)
        cos_ref[...] = jnp.cos(ang)
        sin_ref[...] = jnp.sin(ang) * sign

    acc = jnp.dot(hn_ref[...], w_ref[...], preferred_element_type=F32)

    @pl.when(j < n_rope_tiles)
    def _():
        scale = jnp.where(j < n_q_tiles, 1.0 / math.sqrt(HEAD_DIM), 1.0).astype(F32)
        c = cos_ref[...] * scale
        s = sin_ref[...] * scale
        for hc in range(tn // HEAD_DIM):
            xh = acc[:, hc * HEAD_DIM:(hc + 1) * HEAD_DIM]
            rot = pltpu.roll(xh, HEAD_DIM // 2, 1)
            oq_ref[:, hc * HEAD_DIM:(hc + 1) * HEAD_DIM] = (xh * c + rot * s).astype(BF16)

    @pl.when((j >= n_rope_tiles) & (j < n_qkv_tiles))
    def _():
        oq_ref[...] = acc.astype(BF16)

    @pl.when(j >= n_qkv_tiles)
    def _():
        ou_ref[...] = acc


def _inproj_call(x2, pos2, mod3, invf, w_in_bf, *, seq, attn_w, gmlp_w):
    t, d = x2.shape
    n = w_in_bf.shape[1]
    tm, tn = 1024, 512
    tiles_per_batch = seq // tm
    n_qkv_tiles = 3 * attn_w // tn
    body = functools.partial(_inproj_body, n_q_tiles=attn_w // tn, n_rope_tiles=2 * attn_w // tn,
                             n_qkv_tiles=n_qkv_tiles, tn=tn)
    return pl.pallas_call(
        body,
        grid=(t // tm, n // tn),
        in_specs=[pl.BlockSpec((tm, d), lambda i, j: (i, 0)),
                  pl.BlockSpec((tm, 1), lambda i, j: (i, 0)),
                  pl.BlockSpec((None, 6, d), lambda i, j: (i // tiles_per_batch, 0, 0)),
                  pl.BlockSpec((1, HEAD_DIM), lambda i, j: (0, 0)),
                  pl.BlockSpec((d, tn), lambda i, j: (0, j))],
        out_specs=[pl.BlockSpec((tm, tn), lambda i, j: (i, jnp.minimum(j, n_qkv_tiles - 1))),
                   pl.BlockSpec((tm, tn), lambda i, j: (i, jnp.maximum(j - n_qkv_tiles, 0)))],
        out_shape=[jax.ShapeDtypeStruct((t, 3 * attn_w), BF16),
                   jax.ShapeDtypeStruct((t, 2 * gmlp_w), F32)],
        scratch_shapes=[pltpu.VMEM((tm, d), BF16),
                        pltpu.VMEM((tm, HEAD_DIM), F32),
                        pltpu.VMEM((tm, HEAD_DIM), F32)],
        compiler_params=_cparams(("arbitrary", "arbitrary")),
        name="inproj",
    )(x2, pos2, mod3, invf, w_in_bf)


def _attn_body(q_ref, k_ref, v_ref, o_ref, qs, ks, vs, acc_ref, m_ref, l_ref, *, seq):
    qs[...] = q_ref[...].astype(F32)
    ks[...] = k_ref[...].astype(F32)
    vs[...] = v_ref[...].astype(F32)
    row = lax.broadcasted_iota(jnp.int32, (ATTN_BLOCK, ATTN_BLOCK), 0)
    col = lax.broadcasted_iota(jnp.int32, (ATTN_BLOCK, ATTN_BLOCK), 1)
    ok_prev = col >= row
    ok_cur = col <= row
    contract_last = (((1,), (1,)), ((), ()))

    for branch, dil in enumerate(DILATIONS):
        shift = dil.bit_length() - 1
        span = ATTN_BLOCK * dil

        def rows(start, dil=dil):
            if dil == 1:
                return pl.ds(start, ATTN_BLOCK)
            return pl.ds(start, ATTN_BLOCK, stride=dil)

        def block(idx, carry, dil=dil, shift=shift, span=span, rows=rows, branch=branch):
            r = idx & (dil - 1)
            n = idx >> shift
            cur = rows(r + n * span)
            prev = rows(r + jnp.maximum(n - 1, 0) * span)
            qb = qs[cur, :].astype(BF16)
            lc = lax.dot_general(qb, ks[cur, :].astype(BF16), contract_last, preferred_element_type=F32)
            lp = lax.dot_general(qb, ks[prev, :].astype(BF16), contract_last, preferred_element_type=F32)
            lc = jnp.where(ok_cur, lc, NEG_INF)
            lp = jnp.where(ok_prev & (n > 0), lp, NEG_INF)
            mb = jnp.maximum(jnp.max(lc, axis=1, keepdims=True), jnp.max(lp, axis=1, keepdims=True))
            pc = jnp.exp(lc - mb)
            pp = jnp.exp(lp - mb)
            sb = jnp.sum(pc, axis=1, keepdims=True) + jnp.sum(pp, axis=1, keepdims=True)
            ab = (jnp.dot(pc.astype(BF16), vs[cur, :].astype(BF16), preferred_element_type=F32)
                  + jnp.dot(pp.astype(BF16), vs[prev, :].astype(BF16), preferred_element_type=F32))
            mb = jnp.broadcast_to(mb, (ATTN_BLOCK, HEAD_DIM))
            sb = jnp.broadcast_to(sb, (ATTN_BLOCK, HEAD_DIM))
            if branch == 0:
                acc_ref[cur, :] = ab
                m_ref[cur, :] = mb
                l_ref[cur, :] = sb
            else:
                m_old = m_ref[cur, :]
                m_new = jnp.maximum(m_old, mb)
                w_old = jnp.exp(m_old - m_new)
                w_new = jnp.exp(mb - m_new)
                acc_ref[cur, :] = w_old * acc_ref[cur, :] + w_new * ab
                l_ref[cur, :] = w_old * l_ref[cur, :] + w_new * sb
                m_ref[cur, :] = m_new
            return carry

        lax.fori_loop(0, seq // ATTN_BLOCK, block, 0)

    o_ref[...] = (acc_ref[...] / l_ref[...]).astype(BF16)


def _attn_call(qkv, *, batch, seq, n_heads):
    t = qkv.shape[0]
    blk = lambda off: pl.BlockSpec((seq, HEAD_DIM), lambda b, h: (b, off + h))
    return pl.pallas_call(
        functools.partial(_attn_body, seq=seq),
        grid=(batch, n_heads),
        in_specs=[blk(0), blk(n_heads), blk(2 * n_heads)],
        out_specs=pl.BlockSpec((seq, HEAD_DIM), lambda b, h: (b, h)),
        out_shape=jax.ShapeDtypeStruct((t, n_heads * HEAD_DIM), BF16),
        scratch_shapes=[pltpu.VMEM((seq, HEAD_DIM), F32) for _ in range(6)],
        compiler_params=_cparams(("arbitrary", "arbitrary")),
        name="attn",
    )(qkv, qkv, qkv)


def _mixout_body(x_ref, attn_ref, uv_ref, mod_ref, wout_ref, ws_ref, bsp_ref, gg_ref, gb_ref,
                 l1g_ref, l1b_ref, rw_ref, rb_ref,
                 x1_ref, hp_ref, ri_ref, rf_ref, cnt_ref,
                 gm_ref, carry_ref, rwhi_ref, rwlo_ref, *, tm, attn_w, gmlp_w):
    i = pl.program_id(0)
    d = x_ref.shape[1]

    @pl.when(i == 0)
    def _():
        carry_ref[...] = jnp.zeros_like(carry_ref)
        rw = rw_ref[...]
        hi = rw.astype(BF16)
        rwhi_ref[...] = hi
        rwlo_ref[...] = (rw - hi.astype(F32)).astype(BF16)

    u = _gelu(uv_ref[:, 0:gmlp_w])
    vg = _ln(_gelu(uv_ref[:, gmlp_w:2 * gmlp_w])) * gg_ref[...] + gb_ref[...]
    row = lax.broadcasted_iota(jnp.int32, (GMLP_CHUNK, GMLP_CHUNK), 0)
    col = lax.broadcasted_iota(jnp.int32, (GMLP_CHUNK, GMLP_CHUNK), 1)
    causal = row >= col
    for g in range(gmlp_w // GMLP_GROUP_WIDTH):
        w = jnp.where(causal, ws_ref[g], 0.0).astype(BF16)
        bias = bsp_ref[g * GMLP_CHUNK:(g + 1) * GMLP_CHUNK, :]
        gs = slice(g * GMLP_GROUP_WIDTH, (g + 1) * GMLP_GROUP_WIDTH)
        for ch in range(tm // GMLP_CHUNK):
            rs = slice(ch * GMLP_CHUNK, (ch + 1) * GMLP_CHUNK)
            sp = jnp.dot(w, vg[rs, gs].astype(BF16), preferred_element_type=F32) + bias
            gm_ref[rs, gs] = (u[rs, gs] * sp).astype(BF16)

    mix = (jnp.dot(attn_ref[...], wout_ref[0:attn_w, :], preferred_element_type=F32)
           + jnp.dot(gm_ref[...], wout_ref[attn_w:attn_w + gmlp_w, :], preferred_element_type=F32))
    g1 = mod_ref[2:3, :]
    x1 = _ln(DEEPNORM_ALPHA * x_ref[...] + g1 * mix) * l1g_ref[...] + l1b_ref[...]
    x1_ref[...] = x1

    h2 = _ln(x1) * (1.0 + mod_ref[4:5, :]) + mod_ref[3:4, :]
    hb = h2.astype(BF16)
    half = d // 2
    lo = lax.bitcast_convert_type(hb[:, 0:half].astype(F32), jnp.uint32) >> 16
    hi = lax.bitcast_convert_type(hb[:, half:d].astype(F32), jnp.uint32) & jnp.uint32(0xFFFF0000)
    hp_ref[...] = lo | hi

    h_hi = hb
    h_lo = (h2 - hb.astype(F32)).astype(BF16)
    logits = (jnp.dot(h_hi, rwhi_ref[...], preferred_element_type=F32)
              + jnp.dot(h_lo, rwhi_ref[...], preferred_element_type=F32)
              + jnp.dot(h_hi, rwlo_ref[...], preferred_element_type=F32)) + rb_ref[...]
    lane = lax.broadcasted_iota(jnp.int32, (tm, LANES), 1)
    work = jnp.where(lane < N_EXPERTS, logits, -jnp.inf)
    tops, idxs, sels = [], [], []
    for _ in range(TOP_K):
        m = jnp.max(work, axis=1, keepdims=True)
        idx = jnp.min(jnp.where(work == m, lane, LANES), axis=1, keepdims=True)
        sel = lane == idx
        work = jnp.where(sel, -jnp.inf, work)
        tops.append(m)
        idxs.append(idx)
        sels.append(sel)
    exps = [jnp.exp(m - tops[0]) for m in tops]
    denom = exps[0] + exps[1] + exps[2] + exps[3]
    gates = [e / denom for e in exps]

    selmat = (sels[0] | sels[1] | sels[2] | sels[3]).astype(F32)
    trow = lax.broadcasted_iota(jnp.int32, (tm, tm), 0)
    tcol = lax.broadcasted_iota(jnp.int32, (tm, tm), 1)
    before = (trow > tcol).astype(BF16)
    prefix = jnp.dot(before, selmat.astype(BF16), preferred_element_type=F32) + carry_ref[...]
    ri = jnp.zeros((tm, LANES), jnp.int32)
    rf = jnp.zeros((tm, LANES), F32)
    for k in range(TOP_K):
        rank = jnp.sum(jnp.where(sels[k], prefix, 0.0), axis=1, keepdims=True).astype(jnp.int32)
        ri = ri + jnp.where(lane == k, idxs[k], 0) + jnp.where(lane == TOP_K + k, rank, 0)
        rf = rf + jnp.where(lane == k, gates[k], 0.0)
    ri_ref[...] = ri
    rf_ref[...] = rf
    carry_ref[...] = carry_ref[...] + jnp.sum(selmat, axis=0, keepdims=True)
    cnt_ref[...] = carry_ref[...]


def _mixout_call(x2, attn, uv, mod3, w_out_bf, w_spatial, bsp, gg, gb, l1g, l1b, rw_pad, rb_pad,
                 *, seq, attn_w, gmlp_w):
    t, d = x2.shape
    tm = 256
    tiles_per_batch = seq // tm
    n_groups = gmlp_w // GMLP_GROUP_WIDTH
    const = lambda shape: pl.BlockSpec(shape, lambda i: (0,) * len(shape))
    body = functools.partial(_mixout_body, tm=tm, attn_w=attn_w, gmlp_w=gmlp_w)
    return pl.pallas_call(
        body,
        grid=(t // tm,),
        in_specs=[pl.BlockSpec((tm, d), lambda i: (i, 0)),
                  pl.BlockSpec((tm, attn_w), lambda i: (i, 0)),
                  pl.BlockSpec((tm, 2 * gmlp_w), lambda i: (i, 0)),
                  pl.BlockSpec((None, 6, d), lambda i: (i // tiles_per_batch, 0, 0)),
                  const((attn_w + gmlp_w, d)),
                  const((n_groups, GMLP_CHUNK, GMLP_CHUNK)),
                  const((n_groups * GMLP_CHUNK, 1)),
                  const((1, gmlp_w)), const((1, gmlp_w)),
                  const((1, d)), const((1, d)),
                  const((d, LANES)), const((1, LANES))],
        out_specs=[pl.BlockSpec((tm, d), lambda i: (i, 0)),
                   pl.BlockSpec((tm, d // 2), lambda i: (i, 0)),
                   pl.BlockSpec((tm, LANES), lambda i: (i, 0)),
                   pl.BlockSpec((tm, LANES), lambda i: (i, 0)),
                   pl.BlockSpec((1, LANES), lambda i: (0, 0))],
        out_shape=[jax.ShapeDtypeStruct((t, d), F32),
                   jax.ShapeDtypeStruct((t, d // 2), jnp.uint32),
                   jax.ShapeDtypeStruct((t, LANES), jnp.int32),
                   jax.ShapeDtypeStruct((t, LANES), F32),
                   jax.ShapeDtypeStruct((1, LANES), F32)],
        scratch_shapes=[pltpu.VMEM((tm, gmlp_w), BF16),
                        pltpu.VMEM((1, LANES), F32),
                        pltpu.VMEM((d, LANES), BF16),
                        pltpu.VMEM((d, LANES), BF16)],
        compiler_params=_cparams(("arbitrary",)),
        name="mixout",
    )(x2, attn, uv, mod3, w_out_bf, w_spatial, bsp, gg, gb, l1g, l1b, rw_pad, rb_pad)


def _dispatch_body(fill_ref, pos_ref, hp_ref, xs_ref, zero_ref, sem, fill_sem, *, tb, rows, n_experts, n_blocks):
    i = pl.program_id(0)
    fill_bits = [1 << b for b in range(SUBLANES.bit_length() - 1, rows.bit_length() - 1)]

    def row_copy(src_row, dst_row):
        return pltpu.make_async_copy(hp_ref.at[pl.ds(src_row, 1)], xs_ref.at[pl.ds(dst_row, 1)], sem)

    def zero_copy(dst_row, size):
        return pltpu.make_async_copy(zero_ref.at[pl.ds(0, size)], xs_ref.at[pl.ds(dst_row, size)], fill_sem)

    def fill_padding(wait):
        def expert(e, carry):
            start = fill_ref[e]
            length = fill_ref[n_experts + e]
            head = jnp.minimum((-start) & (SUBLANES - 1), length)
            for h in range(SUBLANES - 1):
                @pl.when(h < head)
                def _(h=h):
                    cp = zero_copy(start + h, 1)
                    cp.wait() if wait else cp.start()
            start = start + head
            length = length - head
            for bit in fill_bits:
                @pl.when((length & bit) != 0)
                def _(bit=bit):
                    cp = zero_copy(pl.multiple_of(start + (length & ~(2 * bit - 1)), SUBLANES), bit)
                    cp.wait() if wait else cp.start()
            return carry

        def unused_block(g, carry):
            cp = zero_copy(g * rows, rows)
            cp.wait() if wait else cp.start()
            return carry

        lax.fori_loop(0, n_experts, expert, 0)
        lax.fori_loop(fill_ref[2 * n_experts], n_blocks, unused_block, 0)

    @pl.when(i == 0)
    def _():
        zero_ref[...] = jnp.zeros_like(zero_ref)
        fill_padding(wait=False)

    def issue(tok, carry):
        for k in range(TOP_K):
            row_copy(i * tb + tok, pos_ref[0, tok * TOP_K + k]).start()
        return carry

    def drain(tok, carry):
        for k in range(TOP_K):
            row_copy(0, 0).wait()
        return carry

    lax.fori_loop(0, tb, issue, 0)
    lax.fori_loop(0, tb, drain, 0)

    @pl.when(i == 0)
    def _():
        fill_padding(wait=True)


def _dispatch_call(fill_meta, pos, hp, *, n_slots, rows, n_experts):
    t, w = hp.shape
    tb = 256
    pos3 = pos.reshape(t // tb, 1, tb * TOP_K)
    grid_spec = pltpu.PrefetchScalarGridSpec(
        num_scalar_prefetch=1,
        grid=(t // tb,),
        in_specs=[pl.BlockSpec((None, 1, tb * TOP_K), lambda i, fm: (i, 0, 0), memory_space=pltpu.SMEM),
                  pl.BlockSpec(memory_space=pl.ANY)],
        out_specs=pl.BlockSpec(memory_space=pl.ANY),
        scratch_shapes=[pltpu.VMEM((rows, w), jnp.uint32),
                        pltpu.SemaphoreType.DMA(()), pltpu.SemaphoreType.DMA(())],
    )
    return pl.pallas_call(
        functools.partial(_dispatch_body, tb=tb, rows=rows, n_experts=n_experts, n_blocks=n_slots // rows),
        grid_spec=grid_spec,
        out_shape=jax.ShapeDtypeStruct((n_slots, w), jnp.uint32),
        compiler_params=_cparams(("arbitrary",)),
        name="dispatch",
    )(fill_meta, pos3, hp)


def _moe_body(be_ref, nv_ref, bm_ref, xs_ref, wg_ref, wu_ref, bg_ref, bu_ref, wd_ref, bd_ref, o_ref,
              xb_ref, *, rows, tm):
    g = pl.program_id(0)
    j = pl.program_id(1)
    nv = nv_ref[g]
    half = xs_ref.shape[1]

    @pl.when((nv == 0) & (j == 0))
    def _():
        o_ref[...] = jnp.zeros_like(o_ref)

    @pl.when(nv > 0)
    def _():
        @pl.when(j == 0)
        def _():
            for s in range(rows // tm):
                rs = slice(s * tm, (s + 1) * tm)
                p = xs_ref[rs, :]
                xb_ref[rs, 0:half] = lax.bitcast_convert_type(p << 16, F32).astype(BF16)
                xb_ref[rs, half:2 * half] = lax.bitcast_convert_type(p & jnp.uint32(0xFFFF0000), F32).astype(BF16)
            o_ref[...] = jnp.broadcast_to(bd_ref[...], o_ref.shape)

        wg = wg_ref[...].astype(BF16)
        wu = wu_ref[...].astype(BF16)
        wd = wd_ref[...].astype(BF16)
        for s in range(rows // tm):
            @pl.when(s * tm < nv)
            def _(s=s):
                rs = slice(s * tm, (s + 1) * tm)
                x = xb_ref[rs, :]
                gate = jnp.dot(x, wg, preferred_element_type=F32) + bg_ref[...]
                up = jnp.dot(x, wu, preferred_element_type=F32) + bu_ref[...]
                gate = jnp.minimum(gate, SWIGLU_LIMIT)
                up = jnp.clip(up, -SWIGLU_LIMIT, SWIGLU_LIMIT)
                act = ((up + 1.0) * (gate * jax.nn.sigmoid(gate * SWIGLU_ALPHA))).astype(BF16)
                o_ref[rs, :] += jnp.dot(act, wd, preferred_element_type=F32)


def _moe_call(block_expert, block_valid, block_map, xs, w_gate_up, b_gate_up3, w_down, b_down3, *, rows):
    n_slots, half = xs.shape
    n_experts, d, two_ff = w_gate_up.shape
    d_ff = two_ff // 2
    tf, tm = 256, 256
    nj = d_ff // tf
    n_blocks = n_slots // rows

    def jj(g, j, nv):
        return jnp.where(nv[g] > 0, j, nj - 1)

    grid_spec = pltpu.PrefetchScalarGridSpec(
        num_scalar_prefetch=3,
        grid=(n_blocks, nj),
        in_specs=[
            pl.BlockSpec((rows, half), lambda g, j, be, nv, bm: (bm[g], 0)),
            pl.BlockSpec((None, d, tf), lambda g, j, be, nv, bm: (be[g], 0, jj(g, j, nv))),
            pl.BlockSpec((None, d, tf), lambda g, j, be, nv, bm: (be[g], 0, nj + jj(g, j, nv))),
            pl.BlockSpec((None, 1, tf), lambda g, j, be, nv, bm: (be[g], 0, jj(g, j, nv))),
            pl.BlockSpec((None, 1, tf), lambda g, j, be, nv, bm: (be[g], 0, nj + jj(g, j, nv))),
            pl.BlockSpec((None, tf, d), lambda g, j, be, nv, bm: (be[g], jj(g, j, nv), 0)),
            pl.BlockSpec((None, 1, d), lambda g, j, be, nv, bm: (be[g], 0, 0)),
        ],
        out_specs=pl.BlockSpec((rows, d), lambda g, j, be, nv, bm: (g, 0)),
        scratch_shapes=[pltpu.VMEM((rows, d), BF16)],
    )
    return pl.pallas_call(
        functools.partial(_moe_body, rows=rows, tm=tm),
        grid_spec=grid_spec,
        out_shape=jax.ShapeDtypeStruct((n_slots, d), F32),
        compiler_params=_cparams(("arbitrary", "arbitrary")),
        name="moe",
    )(block_expert, block_valid, block_map, xs, w_gate_up, w_gate_up, b_gate_up3, b_gate_up3, w_down, b_down3)


def _combine_body(pos_ref, y_ref, x1_ref, rf_ref, mod_ref, l2g_ref, l2b_ref, o_ref, buf_ref, sem, *, tc):
    def row_copy(src_row, k, tok):
        return pltpu.make_async_copy(y_ref.at[pl.ds(src_row, 1)], buf_ref.at[k, pl.ds(tok, 1)], sem)

    def issue(tok, carry):
        for k in range(TOP_K):
            row_copy(pos_ref[0, tok * TOP_K + k], k, tok).start()
        return carry

    def drain(tok, carry):
        for k in range(TOP_K):
            row_copy(0, k, tok).wait()
        return carry

    lax.fori_loop(0, tc, issue, 0)
    lax.fori_loop(0, tc, drain, 0)

    rf = rf_ref[...]
    ffn = rf[:, 0:1] * buf_ref[0]
    for k in range(1, TOP_K):
        ffn = ffn + rf[:, k:k + 1] * buf_ref[k]
    g2 = mod_ref[5:6, :]
    o_ref[...] = _ln(DEEPNORM_ALPHA * x1_ref[...] + g2 * ffn) * l2g_ref[...] + l2b_ref[...]


def _combine_call(pos, y, x1, rf, mod3, l2g, l2b, *, seq):
    t, d = x1.shape
    tc = 128
    tiles_per_batch = seq // tc
    pos3 = pos.reshape(t // tc, 1, tc * TOP_K)
    return pl.pallas_call(
        functools.partial(_combine_body, tc=tc),
        grid=(t // tc,),
        in_specs=[pl.BlockSpec((None, 1, tc * TOP_K), lambda i: (i, 0, 0), memory_space=pltpu.SMEM),
                  pl.BlockSpec(memory_space=pl.ANY),
                  pl.BlockSpec((tc, d), lambda i: (i, 0)),
                  pl.BlockSpec((tc, LANES), lambda i: (i, 0)),
                  pl.BlockSpec((None, 6, d), lambda i: (i // tiles_per_batch, 0, 0)),
                  pl.BlockSpec((1, d), lambda i: (0, 0)),
                  pl.BlockSpec((1, d), lambda i: (0, 0))],
        out_specs=pl.BlockSpec((tc, d), lambda i: (i, 0)),
        out_shape=jax.ShapeDtypeStruct((t, d), F32),
        scratch_shapes=[pltpu.VMEM((TOP_K, tc, d), F32), pltpu.SemaphoreType.DMA(())],
        compiler_params=_cparams(("arbitrary",)),
        name="combine",
    )(pos3, y, x1, rf, mod3, l2g, l2b)


MOE_ROWS = 1024


def _layer(x, c, positions, ada_w, ada_b, w_in, w_spatial, b_spatial, gmlp_ln_g, gmlp_ln_b, w_out,
           ln1_g, ln1_b, router_w, router_b, w_gate_up, b_gate_up, w_down, b_down, ln2_g, ln2_b):
    batch, seq, d = x.shape
    t = batch * seq
    attn_w = ATTN_FRACTION_NUM * d // ATTN_FRACTION_DEN
    gmlp_w = d - attn_w
    n_heads = attn_w // HEAD_DIM
    n_experts = w_gate_up.shape[0]

    c_pad = jnp.pad(c, ((0, 8 - batch), (0, 0)))
    mod = _ada_call(c_pad, ada_w, ada_b.reshape(1, -1))[:batch]
    mod3 = mod.reshape(batch, 6, d)

    half = HEAD_DIM // 2
    inv_freq = jnp.power(ROPE_THETA, -jnp.arange(half, dtype=F32) * (2.0 / HEAD_DIM))
    invf = jnp.concatenate([inv_freq, inv_freq]).reshape(1, HEAD_DIM)
    x2 = x.reshape(t, d)
    qkv, uv = _inproj_call(x2, positions.reshape(t, 1), mod3, invf, w_in.astype(BF16),
                           seq=seq, attn_w=attn_w, gmlp_w=gmlp_w)
    attn = _attn_call(qkv, batch=batch, seq=seq, n_heads=n_heads)

    rw_pad = jnp.pad(router_w, ((0, 0), (0, LANES - n_experts)))
    rb_pad = jnp.pad(router_b.reshape(1, -1), ((0, 0), (0, LANES - n_experts)))
    x1, hp, ri, rf, cnt = _mixout_call(
        x2, attn, uv, mod3, w_out.astype(BF16), w_spatial, b_spatial.reshape(-1, 1),
        gmlp_ln_g.reshape(1, -1), gmlp_ln_b.reshape(1, -1), ln1_g.reshape(1, -1), ln1_b.reshape(1, -1),
        rw_pad, rb_pad, seq=seq, attn_w=attn_w, gmlp_w=gmlp_w)

    counts = cnt[0, :n_experts].astype(jnp.int32)
    padded = (counts + MOE_ROWS - 1) // MOE_ROWS * MOE_ROWS
    pad_end = jnp.cumsum(padded)
    pad_start = pad_end - padded
    pos = pad_start[ri[:, 0:TOP_K]] + ri[:, TOP_K:2 * TOP_K]
    n_blocks = t * TOP_K // MOE_ROWS + n_experts
    n_slots = n_blocks * MOE_ROWS
    block_row = jnp.arange(n_blocks, dtype=jnp.int32) * MOE_ROWS
    block_expert = jnp.minimum(jnp.searchsorted(pad_end, block_row, side='right'), n_experts - 1).astype(jnp.int32)
    block_valid = jnp.clip(counts[block_expert] - (block_row - pad_start[block_expert]), 0, MOE_ROWS).astype(jnp.int32)
    n_used = pad_end[-1] // MOE_ROWS
    block_map = jnp.minimum(jnp.arange(n_blocks, dtype=jnp.int32), n_used - 1).astype(jnp.int32)

    fill_meta = jnp.concatenate([pad_start + counts, padded - counts, n_used[None]]).astype(jnp.int32)
    xs = _dispatch_call(fill_meta, pos, hp, n_slots=n_slots, rows=MOE_ROWS, n_experts=n_experts)
    y = _moe_call(block_expert, block_valid, block_map, xs, w_gate_up, b_gate_up.reshape(n_experts, 1, -1),
                  w_down, b_down.reshape(n_experts, 1, -1), rows=MOE_ROWS)
    out = _combine_call(pos, y, x1, rf, mod3, ln2_g.reshape(1, -1), ln2_b.reshape(1, -1), seq=seq)
    return out.reshape(batch, seq, d)


def kernel(x, c, positions, ada_w, ada_b, w_in, w_spatial, b_spatial, gmlp_ln_g, gmlp_ln_b, w_out, ln1_g, ln1_b, router_w, router_b, w_gate_up, b_gate_up, w_down, b_down, ln2_g, ln2_b):
    assert ada_w.shape[0] == DEPTH
    params = (ada_w, ada_b, w_in, w_spatial, b_spatial, gmlp_ln_g, gmlp_ln_b, w_out, ln1_g, ln1_b,
              router_w, router_b, w_gate_up, b_gate_up, w_down, b_down, ln2_g, ln2_b)
    return _layer(x, c, positions, *[p.reshape(p.shape[1:]) for p in params])
```

```python
import functools
import math

import jax
import jax.numpy as jnp
from jax import lax
from jax.experimental import pallas as pl
from jax.experimental.pallas import tpu as pltpu

HEAD_DIM = 128
ATTN_FRACTION_NUM, ATTN_FRACTION_DEN = 3, 4
GMLP_GROUP_WIDTH = 128
GMLP_CHUNK = 128
DILATIONS = (1, 4, 16)
ATTN_BLOCK = 128
ATTN_UNROLL = 4
ROPE_THETA = 10000.0
N_EXPERTS = 32
TOP_K = 4
SWIGLU_LIMIT = 7.0
SWIGLU_ALPHA = 1.702
DEPTH = 1
DEEPNORM_ALPHA = (2 * DEPTH) ** 0.25
LN_EPS = 1e-5
NEG_INF = -1e30
LANES = 128
SUBLANES = 8

VMEM_LIMIT = 56 * 1024 * 1024

F32 = jnp.float32
BF16 = jnp.bfloat16


def _ln(x):
    mu = jnp.mean(x, axis=-1, keepdims=True)
    xc = x - mu
    return xc * lax.rsqrt(jnp.mean(xc * xc, axis=-1, keepdims=True) + LN_EPS)


def _gelu(x):
    return 0.5 * x * (1.0 + lax.erf(x * (1.0 / math.sqrt(2.0))))


def _cparams(sem):
    return pltpu.CompilerParams(dimension_semantics=sem, vmem_limit_bytes=VMEM_LIMIT)


def _ada_body(c_ref, w_ref, b_ref, o_ref):
    c = c_ref[...]
    s = c * jax.nn.sigmoid(c)
    o_ref[...] = jnp.dot(s, w_ref[...], precision=lax.Precision.HIGHEST,
                         preferred_element_type=F32) + b_ref[...]


def _ada_call(c_pad, ada_w, ada_b):
    rows, d = c_pad.shape
    n = ada_w.shape[1]
    tn = 1024
    return pl.pallas_call(
        _ada_body,
        grid=(n // tn,),
        in_specs=[pl.BlockSpec((rows, d), lambda j: (0, 0)),
                  pl.BlockSpec((d, tn), lambda j: (0, j)),
                  pl.BlockSpec((1, tn), lambda j: (0, j))],
        out_specs=pl.BlockSpec((rows, tn), lambda j: (0, j)),
        out_shape=jax.ShapeDtypeStruct((rows, n), F32),
        compiler_params=_cparams(("arbitrary",)),
        name="ada",
    )(c_pad, ada_w, ada_b)


def _inproj_body(x_ref, pos_ref, mod_ref, invf_ref, w_ref, oq_ref, ou_ref,
                 hn_ref, cos_ref, sin_ref, *, n_q_tiles, n_rope_tiles, n_qkv_tiles, tn):
    j = pl.program_id(1)

    @pl.when(j == 0)
    def _():
        y = _ln(x_ref[...])
        sh = mod_ref[0:1, :]
        sc = mod_ref[1:2, :]
        hn_ref[...] = (y * (1.0 + sc) + sh).astype(BF16)
        ang = pos_ref[...].astype(F32) * invf_ref[...]
        lane = lax.broadcasted_iota(jnp.int32, (1, HEAD_DIM), 1)
        sign = jnp.where(lane < HEAD_DIM // 2, -1.0, 1.0).astype(F32)
        cos_ref[...] = jnp.cos(ang)
        sin_ref[...] = jnp.sin(ang) * sign

    acc = jnp.dot(hn_ref[...], w_ref[...], preferred_element_type=F32)

    @pl.when(j < n_rope_tiles)
    def _():
        scale = jnp.where(j < n_q_tiles, 1.0 / math.sqrt(HEAD_DIM), 1.0).astype(F32)
        c = cos_ref[...] * scale
        s = sin_ref[...] * scale
        for hc in range(tn // HEAD_DIM):
            xh = acc[:, hc * HEAD_DIM:(hc + 1) * HEAD_DIM]
            rot = pltpu.roll(xh, HEAD_DIM // 2, 1)
            oq_ref[:, hc * HEAD_DIM:(hc + 1) * HEAD_DIM] = (xh * c + rot * s).astype(BF16)

    @pl.when((j >= n_rope_tiles) & (j < n_qkv_tiles))
    def _():
        oq_ref[...] = acc.astype(BF16)

    @pl.when(j >= n_qkv_tiles)
    def _():
        ou_ref[...] = acc


def _inproj_call(x2, pos2, mod3, invf, w_in_bf, *, seq, attn_w, gmlp_w):
    t, d = x2.shape
    n = w_in_bf.shape[1]
    tm, tn = 1024, 512
    tiles_per_batch = seq // tm
    n_qkv_tiles = 3 * attn_w // tn
    body = functools.partial(_inproj_body, n_q_tiles=attn_w // tn, n_rope_tiles=2 * attn_w // tn,
                             n_qkv_tiles=n_qkv_tiles, tn=tn)
    return pl.pallas_call(
        body,
        grid=(t // tm, n // tn),
        in_specs=[pl.BlockSpec((tm, d), lambda i, j: (i, 0)),
                  pl.BlockSpec((tm, 1), lambda i, j: (i, 0)),
                  pl.BlockSpec((None, 6, d), lambda i, j: (i // tiles_per_batch, 0, 0)),
                  pl.BlockSpec((1, HEAD_DIM), lambda i, j: (0, 0)),
                  pl.BlockSpec((d, tn), lambda i, j: (0, j))],
        out_specs=[pl.BlockSpec((tm, tn), lambda i, j: (i, jnp.minimum(j, n_qkv_tiles - 1))),
                   pl.BlockSpec((tm, tn), lambda i, j: (i, jnp.maximum(j - n_qkv_tiles, 0)))],
        out_shape=[jax.ShapeDtypeStruct((t, 3 * attn_w), BF16),
                   jax.ShapeDtypeStruct((t, 2 * gmlp_w), F32)],
        scratch_shapes=[pltpu.VMEM((tm, d), BF16),
                        pltpu.VMEM((tm, HEAD_DIM), F32),
                        pltpu.VMEM((tm, HEAD_DIM), F32)],
        compiler_params=_cparams(("arbitrary", "arbitrary")),
        name="inproj",
    )(x2, pos2, mod3, invf, w_in_bf)


def _attn_body(q_ref, k_ref, v_ref, o_ref, qs, ks, vs, acc_ref, m_ref, l_ref, bias_ref, *, seq, unroll):
    qs[...] = q_ref[...].astype(F32)
    ks[...] = k_ref[...].astype(F32)
    vs[...] = v_ref[...].astype(F32)
    window = 2 * ATTN_BLOCK
    delta = (lax.broadcasted_iota(jnp.int32, (ATTN_BLOCK, window), 0)
             - lax.broadcasted_iota(jnp.int32, (ATTN_BLOCK, window), 1))
    for slot in range(2):
        dist = delta + slot * ATTN_BLOCK
        bias_ref[slot] = jnp.where((dist >= 0) & (dist <= ATTN_BLOCK), 0.0, NEG_INF).astype(F32)
    ones = jnp.ones((window, HEAD_DIM), BF16)
    contract_last = (((1,), (1,)), ((), ()))

    for branch, dil in enumerate(DILATIONS):
        shift = dil.bit_length() - 1
        span = ATTN_BLOCK * dil

        def rows(start, size, dil=dil):
            return pl.ds(start, size) if dil == 1 else pl.ds(start, size, stride=dil)

        def group(it, carry, dil=dil, shift=shift, span=span, rows=rows, branch=branch):
            blocks = []
            for u in range(unroll):
                idx = it * unroll + u
                r = idx & (dil - 1)
                n = idx >> shift
                cur = rows(r + n * span, ATTN_BLOCK)
                win = rows(r + jnp.maximum(n - 1, 0) * span, window)
                qb = qs[cur, :].astype(BF16)
                logits = lax.dot_general(qb, ks[win, :].astype(BF16), contract_last, preferred_element_type=F32)
                logits = logits + bias_ref[jnp.minimum(n, 1)]
                mb = jnp.max(logits, axis=1, keepdims=True)
                p = jnp.exp(logits - mb).astype(BF16)
                va = jnp.concatenate([vs[win, :].astype(BF16), ones], axis=1)
                res = jnp.dot(p, va, preferred_element_type=F32)
                blocks.append((cur, res[:, 0:HEAD_DIM], res[:, HEAD_DIM:2 * HEAD_DIM],
                               jnp.broadcast_to(mb, (ATTN_BLOCK, HEAD_DIM))))
            if branch == 0:
                for cur, ab, sb, mb in blocks:
                    acc_ref[cur, :] = ab
                    l_ref[cur, :] = sb
                    m_ref[cur, :] = mb
            else:
                olds = [(m_ref[cur, :], acc_ref[cur, :], l_ref[cur, :]) for cur, _, _, _ in blocks]
                for (cur, ab, sb, mb), (m_old, a_old, l_old) in zip(blocks, olds):
                    m_new = jnp.maximum(m_old, mb)
                    w_old = jnp.exp(m_old - m_new)
                    w_new = jnp.exp(mb - m_new)
                    acc_ref[cur, :] = w_old * a_old + w_new * ab
                    l_ref[cur, :] = w_old * l_old + w_new * sb
                    m_ref[cur, :] = m_new
            return carry

        lax.fori_loop(0, seq // (ATTN_BLOCK * unroll), group, 0)

    o_ref[...] = (acc_ref[...] / l_ref[...]).astype(BF16)


def _attn_call(qkv, *, batch, seq, n_heads):
    t = qkv.shape[0]
    assert seq % (2 * ATTN_BLOCK * max(DILATIONS)) == 0
    assert (seq // ATTN_BLOCK) % ATTN_UNROLL == 0
    blk = lambda off: pl.BlockSpec((seq, HEAD_DIM), lambda b, h: (b, off + h))
    return pl.pallas_call(
        functools.partial(_attn_body, seq=seq, unroll=ATTN_UNROLL),
        grid=(batch, n_heads),
        in_specs=[blk(0), blk(n_heads), blk(2 * n_heads)],
        out_specs=pl.BlockSpec((seq, HEAD_DIM), lambda b, h: (b, h)),
        out_shape=jax.ShapeDtypeStruct((t, n_heads * HEAD_DIM), BF16),
        scratch_shapes=[pltpu.VMEM((seq, HEAD_DIM), F32) for _ in range(6)]
        + [pltpu.VMEM((2, ATTN_BLOCK, 2 * ATTN_BLOCK), F32)],
        compiler_params=_cparams(("arbitrary", "arbitrary")),
        name="attn",
    )(qkv, qkv, qkv)


def _mixout_body(x_ref, attn_ref, uv_ref, mod_ref, wout_ref, ws_ref, bsp_ref, gg_ref, gb_ref,
                 l1g_ref, l1b_ref, rw_ref, rb_ref,
                 x1_ref, hp_ref, ri_ref, rf_ref, cnt_ref,
                 gm_ref, carry_ref, rwhi_ref, rwlo_ref, *, tm, attn_w, gmlp_w):
    i = pl.program_id(0)
    d = x_ref.shape[1]

    @pl.when(i == 0)
    def _():
        carry_ref[...] = jnp.zeros_like(carry_ref)
        rw = rw_ref[...]
        hi = rw.astype(BF16)
        rwhi_ref[...] = hi
        rwlo_ref[...] = (rw - hi.astype(F32)).astype(BF16)

    u = _gelu(uv_ref[:, 0:gmlp_w])
    vg = _ln(_gelu(uv_ref[:, gmlp_w:2 * gmlp_w])) * gg_ref[...] + gb_ref[...]
    row = lax.broadcasted_iota(jnp.int32, (GMLP_CHUNK, GMLP_CHUNK), 0)
    col = lax.broadcasted_iota(jnp.int32, (GMLP_CHUNK, GMLP_CHUNK), 1)
    causal = row >= col
    for g in range(gmlp_w // GMLP_GROUP_WIDTH):
        w = jnp.where(causal, ws_ref[g], 0.0).astype(BF16)
        bias = bsp_ref[g * GMLP_CHUNK:(g + 1) * GMLP_CHUNK, :]
        gs = slice(g * GMLP_GROUP_WIDTH, (g + 1) * GMLP_GROUP_WIDTH)
        for ch in range(tm // GMLP_CHUNK):
            rs = slice(ch * GMLP_CHUNK, (ch + 1) * GMLP_CHUNK)
            sp = jnp.dot(w, vg[rs, gs].astype(BF16), preferred_element_type=F32) + bias
            gm_ref[rs, gs] = (u[rs, gs] * sp).astype(BF16)

    mix = (jnp.dot(attn_ref[...], wout_ref[0:attn_w, :], preferred_element_type=F32)
           + jnp.dot(gm_ref[...], wout_ref[attn_w:attn_w + gmlp_w, :], preferred_element_type=F32))
    g1 = mod_ref[2:3, :]
    x1 = _ln(DEEPNORM_ALPHA * x_ref[...] + g1 * mix) * l1g_ref[...] + l1b_ref[...]
    x1_ref[...] = x1

    h2 = _ln(x1) * (1.0 + mod_ref[4:5, :]) + mod_ref[3:4, :]
    hb = h2.astype(BF16)
    half = d // 2
    lo = lax.bitcast_convert_type(hb[:, 0:half].astype(F32), jnp.uint32) >> 16
    hi = lax.bitcast_convert_type(hb[:, half:d].astype(F32), jnp.uint32) & jnp.uint32(0xFFFF0000)
    hp_ref[...] = lo | hi

    h_hi = hb
    h_lo = (h2 - hb.astype(F32)).astype(BF16)
    logits = (jnp.dot(h_hi, rwhi_ref[...], preferred_element_type=F32)
              + jnp.dot(h_lo, rwhi_ref[...], preferred_element_type=F32)
              + jnp.dot(h_hi, rwlo_ref[...], preferred_element_type=F32)) + rb_ref[...]
    lane = lax.broadcasted_iota(jnp.int32, (tm, LANES), 1)
    work = jnp.where(lane < N_EXPERTS, logits, -jnp.inf)
    tops, idxs, sels = [], [], []
    for _ in range(TOP_K):
        m = jnp.max(work, axis=1, keepdims=True)
        idx = jnp.min(jnp.where(work == m, lane, LANES), axis=1, keepdims=True)
        sel = lane == idx
        work = jnp.where(sel, -jnp.inf, work)
        tops.append(m)
        idxs.append(idx)
        sels.append(sel)
    exps = [jnp.exp(m - tops[0]) for m in tops]
    denom = exps[0] + exps[1] + exps[2] + exps[3]
    gates = [e / denom for e in exps]

    selmat = (sels[0] | sels[1] | sels[2] | sels[3]).astype(F32)
    trow = lax.broadcasted_iota(jnp.int32, (tm, tm), 0)
    tcol = lax.broadcasted_iota(jnp.int32, (tm, tm), 1)
    before = (trow > tcol).astype(BF16)
    prefix = jnp.dot(before, selmat.astype(BF16), preferred_element_type=F32) + carry_ref[...]
    ri = jnp.zeros((tm, LANES), jnp.int32)
    rf = jnp.zeros((tm, LANES), F32)
    for k in range(TOP_K):
        rank = jnp.sum(jnp.where(sels[k], prefix, 0.0), axis=1, keepdims=True).astype(jnp.int32)
        ri = ri + jnp.where(lane == k, idxs[k], 0) + jnp.where(lane == TOP_K + k, rank, 0)
        rf = rf + jnp.where(lane == k, gates[k], 0.0)
    ri_ref[...] = ri
    rf_ref[...] = rf
    carry_ref[...] = carry_ref[...] + jnp.sum(selmat, axis=0, keepdims=True)
    cnt_ref[...] = carry_ref[...]


def _mixout_call(x2, attn, uv, mod3, w_out_bf, w_spatial, bsp, gg, gb, l1g, l1b, rw_pad, rb_pad,
                 *, seq, attn_w, gmlp_w):
    t, d = x2.shape
    tm = 256
    tiles_per_batch = seq // tm
    n_groups = gmlp_w // GMLP_GROUP_WIDTH
    const = lambda shape: pl.BlockSpec(shape, lambda i: (0,) * len(shape))
    body = functools.partial(_mixout_body, tm=tm, attn_w=attn_w, gmlp_w=gmlp_w)
    return pl.pallas_call(
        body,
        grid=(t // tm,),
        in_specs=[pl.BlockSpec((tm, d), lambda i: (i, 0)),
                  pl.BlockSpec((tm, attn_w), lambda i: (i, 0)),
                  pl.BlockSpec((tm, 2 * gmlp_w), lambda i: (i, 0)),
                  pl.BlockSpec((None, 6, d), lambda i: (i // tiles_per_batch, 0, 0)),
                  const((attn_w + gmlp_w, d)),
                  const((n_groups, GMLP_CHUNK, GMLP_CHUNK)),
                  const((n_groups * GMLP_CHUNK, 1)),
                  const((1, gmlp_w)), const((1, gmlp_w)),
                  const((1, d)), const((1, d)),
                  const((d, LANES)), const((1, LANES))],
        out_specs=[pl.BlockSpec((tm, d), lambda i: (i, 0)),
                   pl.BlockSpec((tm, d // 2), lambda i: (i, 0)),
                   pl.BlockSpec((tm, LANES), lambda i: (i, 0)),
                   pl.BlockSpec((tm, LANES), lambda i: (i, 0)),
                   pl.BlockSpec((1, LANES), lambda i: (0, 0))],
        out_shape=[jax.ShapeDtypeStruct((t, d), F32),
                   jax.ShapeDtypeStruct((t, d // 2), jnp.uint32),
                   jax.ShapeDtypeStruct((t, LANES), jnp.int32),
                   jax.ShapeDtypeStruct((t, LANES), F32),
                   jax.ShapeDtypeStruct((1, LANES), F32)],
        scratch_shapes=[pltpu.VMEM((tm, gmlp_w), BF16),
                        pltpu.VMEM((1, LANES), F32),
                        pltpu.VMEM((d, LANES), BF16),
                        pltpu.VMEM((d, LANES), BF16)],
        compiler_params=_cparams(("arbitrary",)),
        name="mixout",
    )(x2, attn, uv, mod3, w_out_bf, w_spatial, bsp, gg, gb, l1g, l1b, rw_pad, rb_pad)


def _dispatch_body(fill_ref, pos_ref, hp_ref, xs_ref, zero_ref, sem, fill_sem, *, tb, rows, n_experts, n_blocks):
    i = pl.program_id(0)
    fill_bits = [1 << b for b in range(SUBLANES.bit_length() - 1, rows.bit_length() - 1)]

    def row_copy(src_row, dst_row):
        return pltpu.make_async_copy(hp_ref.at[pl.ds(src_row, 1)], xs_ref.at[pl.ds(dst_row, 1)], sem)

    def zero_copy(dst_row, size):
        return pltpu.make_async_copy(zero_ref.at[pl.ds(0, size)], xs_ref.at[pl.ds(dst_row, size)], fill_sem)

    def fill_padding(wait):
        def expert(e, carry):
            start = fill_ref[e]
            length = fill_ref[n_experts + e]
            head = jnp.minimum((-start) & (SUBLANES - 1), length)
            for h in range(SUBLANES - 1):
                @pl.when(h < head)
                def _(h=h):
                    cp = zero_copy(start + h, 1)
                    cp.wait() if wait else cp.start()
            start = start + head
            length = length - head
            for bit in fill_bits:
                @pl.when((length & bit) != 0)
                def _(bit=bit):
                    cp = zero_copy(pl.multiple_of(start + (length & ~(2 * bit - 1)), SUBLANES), bit)
                    cp.wait() if wait else cp.start()
            return carry

        def unused_block(g, carry):
            cp = zero_copy(g * rows, rows)
            cp.wait() if wait else cp.start()
            return carry

        lax.fori_loop(0, n_experts, expert, 0)
        lax.fori_loop(fill_ref[2 * n_experts], n_blocks, unused_block, 0)

    @pl.when(i == 0)
    def _():
        zero_ref[...] = jnp.zeros_like(zero_ref)
        fill_padding(wait=False)

    def issue(tok, carry):
        for k in range(TOP_K):
            row_copy(tok, pos_ref[0, tok * TOP_K + k]).start()
        return carry

    def drain(tok, carry):
        for k in range(TOP_K):
            row_copy(0, 0).wait()
        return carry

    lax.fori_loop(0, tb, issue, 0)
    lax.fori_loop(0, tb, drain, 0)

    @pl.when(i == 0)
    def _():
        fill_padding(wait=True)


def _dispatch_call(fill_meta, pos, hp, *, n_slots, rows, n_experts):
    t, w = hp.shape
    tb = 256
    pos3 = pos.reshape(t // tb, 1, tb * TOP_K)
    grid_spec = pltpu.PrefetchScalarGridSpec(
        num_scalar_prefetch=1,
        grid=(t // tb,),
        in_specs=[pl.BlockSpec((None, 1, tb * TOP_K), lambda i, fm: (i, 0, 0), memory_space=pltpu.SMEM),
                  pl.BlockSpec((tb, w), lambda i, fm: (i, 0))],
        out_specs=pl.BlockSpec(memory_space=pl.ANY),
        scratch_shapes=[pltpu.VMEM((rows, w), jnp.uint32),
                        pltpu.SemaphoreType.DMA(()), pltpu.SemaphoreType.DMA(())],
    )
    return pl.pallas_call(
        functools.partial(_dispatch_body, tb=tb, rows=rows, n_experts=n_experts, n_blocks=n_slots // rows),
        grid_spec=grid_spec,
        out_shape=jax.ShapeDtypeStruct((n_slots, w), jnp.uint32),
        compiler_params=_cparams(("arbitrary",)),
        name="dispatch",
    )(fill_meta, pos3, hp)


def _moe_body(be_ref, nv_ref, bm_ref, xs_ref, wg_ref, wu_ref, bg_ref, bu_ref, wd_ref, bd_ref, o_ref,
              xb_ref, *, rows, tm):
    g = pl.program_id(0)
    j = pl.program_id(1)
    nv = nv_ref[g]
    half = xs_ref.shape[1]

    @pl.when((nv == 0) & (j == 0))
    def _():
        o_ref[...] = jnp.zeros_like(o_ref)

    @pl.when(nv > 0)
    def _():
        @pl.when(j == 0)
        def _():
            for s in range(rows // tm):
                rs = slice(s * tm, (s + 1) * tm)
                p = xs_ref[rs, :]
                xb_ref[rs, 0:half] = lax.bitcast_convert_type(p << 16, F32).astype(BF16)
                xb_ref[rs, half:2 * half] = lax.bitcast_convert_type(p & jnp.uint32(0xFFFF0000), F32).astype(BF16)
            o_ref[...] = jnp.broadcast_to(bd_ref[...], o_ref.shape)

        wg = wg_ref[...].astype(BF16)
        wu = wu_ref[...].astype(BF16)
        wd = wd_ref[...].astype(BF16)
        for s in range(rows // tm):
            @pl.when(s * tm < nv)
            def _(s=s):
                rs = slice(s * tm, (s + 1) * tm)
                x = xb_ref[rs, :]
                gate = jnp.dot(x, wg, preferred_element_type=F32) + bg_ref[...]
                up = jnp.dot(x, wu, preferred_element_type=F32) + bu_ref[...]
                gate = jnp.minimum(gate, SWIGLU_LIMIT)
                up = jnp.clip(up, -SWIGLU_LIMIT, SWIGLU_LIMIT)
                act = ((up + 1.0) * (gate * jax.nn.sigmoid(gate * SWIGLU_ALPHA))).astype(BF16)
                o_ref[rs, :] += jnp.dot(act, wd, preferred_element_type=F32)


def _moe_call(block_expert, block_valid, block_map, xs, w_gate_up, b_gate_up3, w_down, b_down3, *, rows):
    n_slots, half = xs.shape
    n_experts, d, two_ff = w_gate_up.shape
    d_ff = two_ff // 2
    tf, tm = 256, 256
    nj = d_ff // tf
    n_blocks = n_slots // rows

    def jj(g, j, nv):
        return jnp.where(nv[g] > 0, j, nj - 1)

    grid_spec = pltpu.PrefetchScalarGridSpec(
        num_scalar_prefetch=3,
        grid=(n_blocks, nj),
        in_specs=[
            pl.BlockSpec((rows, half), lambda g, j, be, nv, bm: (bm[g], 0)),
            pl.BlockSpec((None, d, tf), lambda g, j, be, nv, bm: (be[g], 0, jj(g, j, nv))),
            pl.BlockSpec((None, d, tf), lambda g, j, be, nv, bm: (be[g], 0, nj + jj(g, j, nv))),
            pl.BlockSpec((None, 1, tf), lambda g, j, be, nv, bm: (be[g], 0, jj(g, j, nv))),
            pl.BlockSpec((None, 1, tf), lambda g, j, be, nv, bm: (be[g], 0, nj + jj(g, j, nv))),
            pl.BlockSpec((None, tf, d), lambda g, j, be, nv, bm: (be[g], jj(g, j, nv), 0)),
            pl.BlockSpec((None, 1, d), lambda g, j, be, nv, bm: (be[g], 0, 0)),
        ],
        out_specs=pl.BlockSpec((rows, d), lambda g, j, be, nv, bm: (g, 0)),
        scratch_shapes=[pltpu.VMEM((rows, d), BF16)],
    )
    return pl.pallas_call(
        functools.partial(_moe_body, rows=rows, tm=tm),
        grid_spec=grid_spec,
        out_shape=jax.ShapeDtypeStruct((n_slots, d), F32),
        compiler_params=_cparams(("arbitrary", "arbitrary")),
        name="moe",
    )(block_expert, block_valid, block_map, xs, w_gate_up, w_gate_up, b_gate_up3, b_gate_up3, w_down, b_down3)


def _combine_body(pos_ref, y_ref, x1_ref, rf_ref, mod_ref, l2g_ref, l2b_ref, o_ref, buf_ref, sem, *, tc):
    def row_copy(src_row, k, tok):
        return pltpu.make_async_copy(y_ref.at[pl.ds(src_row, 1)], buf_ref.at[k, pl.ds(tok, 1)], sem)

    def issue(tok, carry):
        for k in range(TOP_K):
            row_copy(pos_ref[0, tok * TOP_K + k], k, tok).start()
        return carry

    def drain(tok, carry):
        for k in range(TOP_K):
            row_copy(0, k, tok).wait()
        return carry

    lax.fori_loop(0, tc, issue, 0)
    lax.fori_loop(0, tc, drain, 0)

    rf = rf_ref[...]
    ffn = rf[:, 0:1] * buf_ref[0]
    for k in range(1, TOP_K):
        ffn = ffn + rf[:, k:k + 1] * buf_ref[k]
    g2 = mod_ref[5:6, :]
    o_ref[...] = _ln(DEEPNORM_ALPHA * x1_ref[...] + g2 * ffn) * l2g_ref[...] + l2b_ref[...]


def _combine_call(pos, y, x1, rf, mod3, l2g, l2b, *, seq):
    t, d = x1.shape
    tc = 128
    tiles_per_batch = seq // tc
    pos3 = pos.reshape(t // tc, 1, tc * TOP_K)
    return pl.pallas_call(
        functools.partial(_combine_body, tc=tc),
        grid=(t // tc,),
        in_specs=[pl.BlockSpec((None, 1, tc * TOP_K), lambda i: (i, 0, 0), memory_space=pltpu.SMEM),
                  pl.BlockSpec(memory_space=pl.ANY),
                  pl.BlockSpec((tc, d), lambda i: (i, 0)),
                  pl.BlockSpec((tc, LANES), lambda i: (i, 0)),
                  pl.BlockSpec((None, 6, d), lambda i: (i // tiles_per_batch, 0, 0)),
                  pl.BlockSpec((1, d), lambda i: (0, 0)),
                  pl.BlockSpec((1, d), lambda i: (0, 0))],
        out_specs=pl.BlockSpec((tc, d), lambda i: (i, 0)),
        out_shape=jax.ShapeDtypeStruct((t, d), F32),
        scratch_shapes=[pltpu.VMEM((TOP_K, tc, d), F32), pltpu.SemaphoreType.DMA(())],
        compiler_params=_cparams(("arbitrary",)),
        name="combine",
    )(pos3, y, x1, rf, mod3, l2g, l2b)


MOE_ROWS = 1024


def _layer(x, c, positions, ada_w, ada_b, w_in, w_spatial, b_spatial, gmlp_ln_g, gmlp_ln_b, w_out,
           ln1_g, ln1_b, router_w, router_b, w_gate_up, b_gate_up, w_down, b_down, ln2_g, ln2_b):
    batch, seq, d = x.shape
    t = batch * seq
    attn_w = ATTN_FRACTION_NUM * d // ATTN_FRACTION_DEN
    gmlp_w = d - attn_w
    n_heads = attn_w // HEAD_DIM
    n_experts = w_gate_up.shape[0]

    c_pad = jnp.pad(c, ((0, 8 - batch), (0, 0)))
    mod = _ada_call(c_pad, ada_w, ada_b.reshape(1, -1))[:batch]
    mod3 = mod.reshape(batch, 6, d)

    half = HEAD_DIM // 2
    inv_freq = jnp.power(ROPE_THETA, -jnp.arange(half, dtype=F32) * (2.0 / HEAD_DIM))
    invf = jnp.concatenate([inv_freq, inv_freq]).reshape(1, HEAD_DIM)
    x2 = x.reshape(t, d)
    qkv, uv = _inproj_call(x2, positions.reshape(t, 1), mod3, invf, w_in.astype(BF16),
                           seq=seq, attn_w=attn_w, gmlp_w=gmlp_w)
    attn = _attn_call(qkv, batch=batch, seq=seq, n_heads=n_heads)

    rw_pad = jnp.pad(router_w, ((0, 0), (0, LANES - n_experts)))
    rb_pad = jnp.pad(router_b.reshape(1, -1), ((0, 0), (0, LANES - n_experts)))
    x1, hp, ri, rf, cnt = _mixout_call(
        x2, attn, uv, mod3, w_out.astype(BF16), w_spatial, b_spatial.reshape(-1, 1),
        gmlp_ln_g.reshape(1, -1), gmlp_ln_b.reshape(1, -1), ln1_g.reshape(1, -1), ln1_b.reshape(1, -1),
        rw_pad, rb_pad, seq=seq, attn_w=attn_w, gmlp_w=gmlp_w)

    counts = cnt[0, :n_experts].astype(jnp.int32)
    padded = (counts + MOE_ROWS - 1) // MOE_ROWS * MOE_ROWS
    pad_end = jnp.cumsum(padded)
    pad_start = pad_end - padded
    pos = pad_start[ri[:, 0:TOP_K]] + ri[:, TOP_K:2 * TOP_K]
    n_blocks = t * TOP_K // MOE_ROWS + n_experts
    n_slots = n_blocks * MOE_ROWS
    block_row = jnp.arange(n_blocks, dtype=jnp.int32) * MOE_ROWS
    block_expert = jnp.minimum(jnp.sum(pad_end[None, :] <= block_row[:, None], axis=1), n_experts - 1).astype(jnp.int32)
    block_valid = jnp.clip(counts[block_expert] - (block_row - pad_start[block_expert]), 0, MOE_ROWS).astype(jnp.int32)
    n_used = pad_end[-1] // MOE_ROWS
    block_map = jnp.minimum(jnp.arange(n_blocks, dtype=jnp.int32), n_used - 1).astype(jnp.int32)

    fill_meta = jnp.concatenate([pad_start + counts, padded - counts, n_used[None]]).astype(jnp.int32)
    xs = _dispatch_call(fill_meta, pos, hp, n_slots=n_slots, rows=MOE_ROWS, n_experts=n_experts)
    y = _moe_call(block_expert, block_valid, block_map, xs, w_gate_up, b_gate_up.reshape(n_experts, 1, -1),
                  w_down, b_down.reshape(n_experts, 1, -1), rows=MOE_ROWS)
    out = _combine_call(pos, y, x1, rf, mod3, ln2_g.reshape(1, -1), ln2_b.reshape(1, -1), seq=seq)
    return out.reshape(batch, seq, d)


def kernel(x, c, positions, ada_w, ada_b, w_in, w_spatial, b_spatial, gmlp_ln_g, gmlp_ln_b, w_out, ln1_g, ln1_b, router_w, router_b, w_gate_up, b_gate_up, w_down, b_down, ln2_g, ln2_b):
    assert ada_w.shape[0] == DEPTH
    params = (ada_w, ada_b, w_in, w_spatial, b_spatial, gmlp_ln_g, gmlp_ln_b, w_out, ln1_g, ln1_b,
              router_w, router_b, w_gate_up, b_gate_up, w_down, b_down, ln2_g, ln2_b)
    return _layer(x, c, positions, *[p.reshape(p.shape[1:]) for p in params])
```

```python
import functools
import math

import jax
import jax.numpy as jnp
from jax import lax
from jax.experimental import pallas as pl
from jax.experimental.pallas import tpu as pltpu

HEAD_DIM = 128
ATTN_FRACTION_NUM, ATTN_FRACTION_DEN = 3, 4
GMLP_GROUP_WIDTH = 128
GMLP_CHUNK = 128
DILATIONS = (1, 4, 16)
ATTN_BLOCK = 128
ATTN_UNROLL = 4
ROPE_THETA = 10000.0
N_EXPERTS = 32
TOP_K = 4
SWIGLU_LIMIT = 7.0
SWIGLU_ALPHA = 1.702
DEPTH = 1
DEEPNORM_ALPHA = (2 * DEPTH) ** 0.25
LN_EPS = 1e-5
NEG_INF = -1e30
LANES = 128
SUBLANES = 8
MXU_WIDTH = 256

VMEM_LIMIT = 56 * 1024 * 1024
MOE_VMEM_LIMIT = 60 * 1024 * 1024
COMBINE_TILE = 128

F32 = jnp.float32
BF16 = jnp.bfloat16


def _ln(x):
    mu = jnp.mean(x, axis=-1, keepdims=True)
    xc = x - mu
    return xc * lax.rsqrt(jnp.mean(xc * xc, axis=-1, keepdims=True) + LN_EPS)


def _gelu(x):
    return 0.5 * x * (1.0 + lax.erf(x * (1.0 / math.sqrt(2.0))))


def _cparams(sem):
    return pltpu.CompilerParams(dimension_semantics=sem, vmem_limit_bytes=VMEM_LIMIT)


def _ada_body(c_ref, w_ref, b_ref, o_ref):
    c = c_ref[...]
    s = c * jax.nn.sigmoid(c)
    o_ref[...] = jnp.dot(s, w_ref[...], precision=lax.Precision.HIGHEST,
                         preferred_element_type=F32) + b_ref[...]


def _ada_call(c_pad, ada_w, ada_b):
    rows, d = c_pad.shape
    n = ada_w.shape[1]
    tn = 1024
    return pl.pallas_call(
        _ada_body,
        grid=(n // tn,),
        in_specs=[pl.BlockSpec((rows, d), lambda j: (0, 0)),
                  pl.BlockSpec((d, tn), lambda j: (0, j)),
                  pl.BlockSpec((1, tn), lambda j: (0, j))],
        out_specs=pl.BlockSpec((rows, tn), lambda j: (0, j)),
        out_shape=jax.ShapeDtypeStruct((rows, n), F32),
        compiler_params=_cparams(("arbitrary",)),
        name="ada",
    )(c_pad, ada_w, ada_b)


def _inproj_body(x_ref, pos_ref, mod_ref, invf_ref, w_ref, oq_ref, ou_ref,
                 hn_ref, cos_ref, sin_ref, *, n_q_tiles, n_rope_tiles, n_qkv_tiles, tn):
    j = pl.program_id(1)

    @pl.when(j == 0)
    def _():
        y = _ln(x_ref[...])
        sh = mod_ref[0:1, :]
        sc = mod_ref[1:2, :]
        hn_ref[...] = (y * (1.0 + sc) + sh).astype(BF16)
        ang = pos_ref[...].astype(F32) * invf_ref[...]
        lane = lax.broadcasted_iota(jnp.int32, (1, HEAD_DIM), 1)
        sign = jnp.where(lane < HEAD_DIM // 2, -1.0, 1.0).astype(F32)
        cos_ref[...] = jnp.cos(ang)
        sin_ref[...] = jnp.sin(ang) * sign

    acc = jnp.dot(hn_ref[...], w_ref[...], preferred_element_type=F32)

    @pl.when(j < n_rope_tiles)
    def _():
        scale = jnp.where(j < n_q_tiles, 1.0 / math.sqrt(HEAD_DIM), 1.0).astype(F32)
        c = cos_ref[...] * scale
        s = sin_ref[...] * scale
        for hc in range(tn // HEAD_DIM):
            xh = acc[:, hc * HEAD_DIM:(hc + 1) * HEAD_DIM]
            rot = pltpu.roll(xh, HEAD_DIM // 2, 1)
            oq_ref[:, hc * HEAD_DIM:(hc + 1) * HEAD_DIM] = (xh * c + rot * s).astype(BF16)

    @pl.when((j >= n_rope_tiles) & (j < n_qkv_tiles))
    def _():
        oq_ref[...] = acc.astype(BF16)

    @pl.when(j >= n_qkv_tiles)
    def _():
        ou_ref[...] = acc


def _inproj_call(x2, pos2, mod3, invf, w_in_bf, *, seq, attn_w, gmlp_w):
    t, d = x2.shape
    n = w_in_bf.shape[1]
    tm, tn = 1024, 512
    tiles_per_batch = seq // tm
    n_qkv_tiles = 3 * attn_w // tn
    body = functools.partial(_inproj_body, n_q_tiles=attn_w // tn, n_rope_tiles=2 * attn_w // tn,
                             n_qkv_tiles=n_qkv_tiles, tn=tn)
    return pl.pallas_call(
        body,
        grid=(t // tm, n // tn),
        in_specs=[pl.BlockSpec((tm, d), lambda i, j: (i, 0)),
                  pl.BlockSpec((tm, 1), lambda i, j: (i, 0)),
                  pl.BlockSpec((None, 6, d), lambda i, j: (i // tiles_per_batch, 0, 0)),
                  pl.BlockSpec((1, HEAD_DIM), lambda i, j: (0, 0)),
                  pl.BlockSpec((d, tn), lambda i, j: (0, j))],
        out_specs=[pl.BlockSpec((tm, tn), lambda i, j: (i, jnp.minimum(j, n_qkv_tiles - 1))),
                   pl.BlockSpec((tm, tn), lambda i, j: (i, jnp.maximum(j - n_qkv_tiles, 0)))],
        out_shape=[jax.ShapeDtypeStruct((t, 3 * attn_w), BF16),
                   jax.ShapeDtypeStruct((t, 2 * gmlp_w), F32)],
        scratch_shapes=[pltpu.VMEM((tm, d), BF16),
                        pltpu.VMEM((tm, HEAD_DIM), F32),
                        pltpu.VMEM((tm, HEAD_DIM), F32)],
        compiler_params=_cparams(("arbitrary", "arbitrary")),
        name="inproj",
    )(x2, pos2, mod3, invf, w_in_bf)


def _attn_body(q_ref, k_ref, v_ref, o_ref, qs, ks, vs, acc_ref, m_ref, l_ref, bias_ref, *, seq, unroll):
    qs[...] = q_ref[...].astype(F32)
    ks[...] = k_ref[...].astype(F32)
    vs[...] = v_ref[...].astype(F32)
    window = 2 * ATTN_BLOCK
    delta = (lax.broadcasted_iota(jnp.int32, (ATTN_BLOCK, window), 0)
             - lax.broadcasted_iota(jnp.int32, (ATTN_BLOCK, window), 1))
    for slot in range(2):
        dist = delta + slot * ATTN_BLOCK
        bias_ref[slot] = jnp.where((dist >= 0) & (dist <= ATTN_BLOCK), 0.0, NEG_INF).astype(F32)
    ones = jnp.ones((window, HEAD_DIM), BF16)
    contract_last = (((1,), (1,)), ((), ()))

    for branch, dil in enumerate(DILATIONS):
        shift = dil.bit_length() - 1
        span = ATTN_BLOCK * dil

        def rows(start, size, dil=dil):
            return pl.ds(start, size) if dil == 1 else pl.ds(start, size, stride=dil)

        def group(it, carry, dil=dil, shift=shift, span=span, rows=rows, branch=branch):
            blocks = []
            for u in range(unroll):
                idx = it * unroll + u
                r = idx & (dil - 1)
                n = idx >> shift
                cur = rows(r + n * span, ATTN_BLOCK)
                win = rows(r + jnp.maximum(n - 1, 0) * span, window)
                qb = qs[cur, :].astype(BF16)
                logits = lax.dot_general(qb, ks[win, :].astype(BF16), contract_last, preferred_element_type=F32)
                logits = logits + bias_ref[jnp.minimum(n, 1)]
                mb = jnp.max(logits, axis=1, keepdims=True)
                p = jnp.exp(logits - mb).astype(BF16)
                va = jnp.concatenate([vs[win, :].astype(BF16), ones], axis=1)
                res = jnp.dot(p, va, preferred_element_type=F32)
                blocks.append((cur, res[:, 0:HEAD_DIM], res[:, HEAD_DIM:2 * HEAD_DIM],
                               jnp.broadcast_to(mb, (ATTN_BLOCK, HEAD_DIM))))
            if branch == 0:
                for cur, ab, sb, mb in blocks:
                    acc_ref[cur, :] = ab
                    l_ref[cur, :] = sb
                    m_ref[cur, :] = mb
            else:
                olds = [(m_ref[cur, :], acc_ref[cur, :], l_ref[cur, :]) for cur, _, _, _ in blocks]
                for (cur, ab, sb, mb), (m_old, a_old, l_old) in zip(blocks, olds):
                    m_new = jnp.maximum(m_old, mb)
                    w_old = jnp.exp(m_old - m_new)
                    w_new = jnp.exp(mb - m_new)
                    acc_ref[cur, :] = w_old * a_old + w_new * ab
                    l_ref[cur, :] = w_old * l_old + w_new * sb
                    m_ref[cur, :] = m_new
            return carry

        lax.fori_loop(0, seq // (ATTN_BLOCK * unroll), group, 0)

    o_ref[...] = (acc_ref[...] / l_ref[...]).astype(BF16)


def _attn_call(qkv, *, batch, seq, n_heads):
    t = qkv.shape[0]
    assert seq % (2 * ATTN_BLOCK * max(DILATIONS)) == 0
    assert (seq // ATTN_BLOCK) % ATTN_UNROLL == 0
    blk = lambda off: pl.BlockSpec((seq, HEAD_DIM), lambda b, h: (b, off + h))
    return pl.pallas_call(
        functools.partial(_attn_body, seq=seq, unroll=ATTN_UNROLL),
        grid=(batch, n_heads),
        in_specs=[blk(0), blk(n_heads), blk(2 * n_heads)],
        out_specs=pl.BlockSpec((seq, HEAD_DIM), lambda b, h: (b, h)),
        out_shape=jax.ShapeDtypeStruct((t, n_heads * HEAD_DIM), BF16),
        scratch_shapes=[pltpu.VMEM((seq, HEAD_DIM), F32) for _ in range(6)]
        + [pltpu.VMEM((2, ATTN_BLOCK, 2 * ATTN_BLOCK), F32)],
        compiler_params=_cparams(("arbitrary", "arbitrary")),
        name="attn",
    )(qkv, qkv, qkv)


def _mixout_body(x_ref, attn_ref, uv_ref, mod_ref, wout_ref, ws_ref, bsp_ref, gg_ref, gb_ref,
                 l1g_ref, l1b_ref, rw_ref, rb_ref,
                 x1_ref, hp_ref, ri_ref, rf_ref, cnt_ref,
                 gm_ref, carry_ref, rwhi_ref, rwlo_ref, *, tm, attn_w, gmlp_w):
    i = pl.program_id(0)
    d = x_ref.shape[1]

    @pl.when(i == 0)
    def _():
        carry_ref[...] = jnp.zeros_like(carry_ref)
        rw = rw_ref[...]
        hi = rw.astype(BF16)
        rwhi_ref[...] = hi
        rwlo_ref[...] = (rw - hi.astype(F32)).astype(BF16)

    u = _gelu(uv_ref[:, 0:gmlp_w])
    vg = _ln(_gelu(uv_ref[:, gmlp_w:2 * gmlp_w])) * gg_ref[...] + gb_ref[...]
    row = lax.broadcasted_iota(jnp.int32, (GMLP_CHUNK, GMLP_CHUNK), 0)
    col = lax.broadcasted_iota(jnp.int32, (GMLP_CHUNK, GMLP_CHUNK), 1)
    causal = row >= col
    for g in range(gmlp_w // GMLP_GROUP_WIDTH):
        w = jnp.where(causal, ws_ref[g], 0.0).astype(BF16)
        bias = bsp_ref[g * GMLP_CHUNK:(g + 1) * GMLP_CHUNK, :]
        gs = slice(g * GMLP_GROUP_WIDTH, (g + 1) * GMLP_GROUP_WIDTH)
        for ch in range(tm // GMLP_CHUNK):
            rs = slice(ch * GMLP_CHUNK, (ch + 1) * GMLP_CHUNK)
            sp = jnp.dot(w, vg[rs, gs].astype(BF16), preferred_element_type=F32) + bias
            gm_ref[rs, gs] = (u[rs, gs] * sp).astype(BF16)

    mix = (jnp.dot(attn_ref[...], wout_ref[0:attn_w, :], preferred_element_type=F32)
           + jnp.dot(gm_ref[...], wout_ref[attn_w:attn_w + gmlp_w, :], preferred_element_type=F32))
    g1 = mod_ref[2:3, :]
    x1 = _ln(DEEPNORM_ALPHA * x_ref[...] + g1 * mix) * l1g_ref[...] + l1b_ref[...]
    x1_ref[...] = x1

    h2 = _ln(x1) * (1.0 + mod_ref[4:5, :]) + mod_ref[3:4, :]
    hp_ref[...] = h2
    hb = h2.astype(BF16)

    h_hi = hb
    h_lo = (h2 - hb.astype(F32)).astype(BF16)
    logits = (jnp.dot(h_hi, rwhi_ref[...], preferred_element_type=F32)
              + jnp.dot(h_lo, rwhi_ref[...], preferred_element_type=F32)
              + jnp.dot(h_hi, rwlo_ref[...], preferred_element_type=F32)) + rb_ref[...]
    lane = lax.broadcasted_iota(jnp.int32, (tm, LANES), 1)
    work = jnp.where(lane < N_EXPERTS, logits, -jnp.inf)
    tops, idxs, sels = [], [], []
    for _ in range(TOP_K):
        m = jnp.max(work, axis=1, keepdims=True)
        idx = jnp.min(jnp.where(work == m, lane, LANES), axis=1, keepdims=True)
        sel = lane == idx
        work = jnp.where(sel, -jnp.inf, work)
        tops.append(m)
        idxs.append(idx)
        sels.append(sel)
    exps = [jnp.exp(m - tops[0]) for m in tops]
    denom = exps[0] + exps[1] + exps[2] + exps[3]
    gates = [e / denom for e in exps]

    selmat = (sels[0] | sels[1] | sels[2] | sels[3]).astype(F32)
    trow = lax.broadcasted_iota(jnp.int32, (tm, tm), 0)
    tcol = lax.broadcasted_iota(jnp.int32, (tm, tm), 1)
    before = (trow > tcol).astype(BF16)
    prefix = jnp.dot(before, selmat.astype(BF16), preferred_element_type=F32) + carry_ref[...]
    ri = jnp.zeros((tm, LANES), jnp.int32)
    rf = jnp.zeros((tm, LANES), F32)
    for k in range(TOP_K):
        rank = jnp.sum(jnp.where(sels[k], prefix, 0.0), axis=1, keepdims=True).astype(jnp.int32)
        ri = ri + jnp.where(lane == k, idxs[k], 0) + jnp.where(lane == TOP_K + k, rank, 0)
        rf = rf + jnp.where(lane == k, gates[k], 0.0)
    ri_ref[...] = ri
    rf_ref[...] = rf
    carry_ref[...] = carry_ref[...] + jnp.sum(selmat, axis=0, keepdims=True)
    cnt_ref[...] = carry_ref[...]


def _mixout_call(x2, attn, uv, mod3, w_out_bf, w_spatial, bsp, gg, gb, l1g, l1b, rw_pad, rb_pad,
                 *, seq, attn_w, gmlp_w):
    t, d = x2.shape
    tm = 256
    tiles_per_batch = seq // tm
    n_groups = gmlp_w // GMLP_GROUP_WIDTH
    const = lambda shape: pl.BlockSpec(shape, lambda i: (0,) * len(shape))
    body = functools.partial(_mixout_body, tm=tm, attn_w=attn_w, gmlp_w=gmlp_w)
    return pl.pallas_call(
        body,
        grid=(t // tm,),
        in_specs=[pl.BlockSpec((tm, d), lambda i: (i, 0)),
                  pl.BlockSpec((tm, attn_w), lambda i: (i, 0)),
                  pl.BlockSpec((tm, 2 * gmlp_w), lambda i: (i, 0)),
                  pl.BlockSpec((None, 6, d), lambda i: (i // tiles_per_batch, 0, 0)),
                  const((attn_w + gmlp_w, d)),
                  const((n_groups, GMLP_CHUNK, GMLP_CHUNK)),
                  const((n_groups * GMLP_CHUNK, 1)),
                  const((1, gmlp_w)), const((1, gmlp_w)),
                  const((1, d)), const((1, d)),
                  const((d, LANES)), const((1, LANES))],
        out_specs=[pl.BlockSpec((tm, d), lambda i: (i, 0)),
                   pl.BlockSpec((tm, d), lambda i: (i, 0)),
                   pl.BlockSpec((tm, LANES), lambda i: (i, 0)),
                   pl.BlockSpec((tm, LANES), lambda i: (i, 0)),
                   pl.BlockSpec((1, LANES), lambda i: (0, 0))],
        out_shape=[jax.ShapeDtypeStruct((t, d), F32),
                   jax.ShapeDtypeStruct((t, d), F32),
                   jax.ShapeDtypeStruct((t, LANES), jnp.int32),
                   jax.ShapeDtypeStruct((t, LANES), F32),
                   jax.ShapeDtypeStruct((1, LANES), F32)],
        scratch_shapes=[pltpu.VMEM((tm, gmlp_w), BF16),
                        pltpu.VMEM((1, LANES), F32),
                        pltpu.VMEM((d, LANES), BF16),
                        pltpu.VMEM((d, LANES), BF16)],
        compiler_params=_cparams(("arbitrary",)),
        name="mixout",
    )(x2, attn, uv, mod3, w_out_bf, w_spatial, bsp, gg, gb, l1g, l1b, rw_pad, rb_pad)


def _slotmap_body(pos_ref, ent_ref, init_ref, tab_ref, *, chunk):
    @pl.when(pl.program_id(0) == 0)
    def _():
        pltpu.sync_copy(init_ref, tab_ref)

    def entry(a, carry):
        tab_ref[pos_ref[0, a]] = ent_ref[0, a]
        return carry

    lax.fori_loop(0, chunk, entry, 0)


def _slotmap_call(pos, entries, init):
    chunk = 4096
    n = pos.size
    smem = lambda: pl.BlockSpec((None, 1, chunk), lambda i: (i, 0, 0), memory_space=pltpu.SMEM)
    return pl.pallas_call(
        functools.partial(_slotmap_body, chunk=chunk),
        grid=(n // chunk,),
        in_specs=[smem(), smem(), pl.BlockSpec(memory_space=pl.ANY)],
        out_specs=pl.BlockSpec(memory_space=pltpu.SMEM),
        out_shape=jax.ShapeDtypeStruct(init.shape, jnp.int32),
        compiler_params=_cparams(("arbitrary",)),
        name="slotmap",
    )(pos.reshape(n // chunk, 1, chunk), entries.reshape(n // chunk, 1, chunk), init)


def _moe_body(be_ref, nv_ref, tab0_ref, tabn_ref, tabp_ref, hp_ref, wg_ref, wu_ref, bg_ref, bu_ref, wd_ref,
              bd_ref, yt_ref, xg_ref, xb_ref, act_ref, ys_ref, gsem, ssem, zsem,
              *, rows, tm, nj, nd, tn, tok_bits, n_real):
    g = pl.program_id(0)
    s = pl.program_id(1)
    nv = nv_ref[g]
    prev_used = (g > 0) & (nv_ref[jnp.maximum(g - 1, 0)] > 0)
    tok_mask = (1 << tok_bits) - 1
    scatter_a = rows // nj
    gather_a = rows // (2 * nj)
    gather_b = rows // (2 * nd)
    halves = tn // MXU_WIDTH
    assert tn == wg_ref.shape[1] and tn % MXU_WIDTH == 0

    def gather_copy(entry, row):
        return pltpu.make_async_copy(hp_ref.at[pl.ds(entry & tok_mask, 1)], xg_ref.at[pl.ds(row, 1)], gsem)

    def scatter_copy(entry, row):
        return pltpu.make_async_copy(ys_ref.at[pl.ds(row, 1)], yt_ref.at[pl.ds(entry >> tok_bits, 1)], ssem)

    def wait_gather():
        pltpu.make_async_copy(hp_ref.at[pl.ds(0, rows)], xg_ref, gsem).wait()

    def wait_scatter():
        pltpu.make_async_copy(ys_ref, yt_ref.at[pl.ds(0, rows)], ssem).wait()

    @pl.when((g == 0) & (s == 0))
    def _():
        ys_ref[...] = jnp.zeros_like(ys_ref)
        dump = pltpu.make_async_copy(ys_ref, yt_ref.at[pl.ds(n_real, rows)], zsem)
        dump.start()
        dump.wait()

        def first(row, carry):
            gather_copy(tab0_ref[0, row], row).start()
            return carry

        lax.fori_loop(0, rows, first, 0)

    @pl.when((s == 0) & ((g == 0) | prev_used))
    def _():
        wait_gather()

    @pl.when((nv == 0) & prev_used & (s == 0))
    def _():
        def last(row, carry):
            scatter_copy(tabp_ref[0, row], row).start()
            return carry

        lax.fori_loop(0, rows, last, 0)
        wait_scatter()

    @pl.when(nv > 0)
    def _():
        @pl.when(s == 0)
        def _():
            for r in range(rows // tm):
                rs = slice(r * tm, (r + 1) * tm)
                xb_ref[rs, :] = xg_ref[rs, :].astype(BF16)

        def batches(first_scatter, n_scatter, first_gather, n_gather, pieces):
            def batch(q):
                for i in range(q * n_scatter // pieces, (q + 1) * n_scatter // pieces):
                    scatter_copy(tabp_ref[0, first_scatter + i], first_scatter + i).start()
                for i in range(q * n_gather // pieces, (q + 1) * n_gather // pieces):
                    gather_copy(tabn_ref[0, first_gather + i], first_gather + i).start()
            return [functools.partial(batch, q) for q in range(pieces)]

        for a in range(nj):
            @pl.when(s == a)
            def _(a=a):
                wg = wg_ref[...].astype(BF16)
                wu = wu_ref[...].astype(BF16)
                issue = batches(a * scatter_a, scatter_a, a * gather_a, gather_a, 2 * halves)

                def sub_tile(r, issue=()):
                    rs = slice(r * tm, (r + 1) * tm)
                    x = xb_ref[rs, :]
                    for h in range(halves):
                        cs = slice(h * MXU_WIDTH, (h + 1) * MXU_WIDTH)
                        gate = jnp.dot(x, wg[:, cs], preferred_element_type=F32) + bg_ref[:, cs]
                        if issue:
                            issue[2 * h]()
                        up = jnp.dot(x, wu[:, cs], preferred_element_type=F32) + bu_ref[:, cs]
                        if issue:
                            issue[2 * h + 1]()
                        gate = jnp.minimum(gate, SWIGLU_LIMIT)
                        up = jnp.clip(up, -SWIGLU_LIMIT, SWIGLU_LIMIT)
                        act_ref[a, rs, cs] = ((up + 1.0) * (gate * jax.nn.sigmoid(gate * SWIGLU_ALPHA))).astype(BF16)

                sub_tile(0, issue)
                for r in range(1, rows // tm):
                    pl.when(r * tm < nv)(functools.partial(sub_tile, r))

        @pl.when(s == nj)
        def _():
            wait_scatter()

        for c in range(nd):
            @pl.when(s == nj + c)
            def _(c=c):
                wd = wd_ref[...].astype(BF16)
                issue = batches(0, 0, nj * gather_a + c * gather_b, gather_b, halves)

                def sub_tile(r, issue=()):
                    rs = slice(r * tm, (r + 1) * tm)
                    a = jnp.concatenate([act_ref[j, rs, :] for j in range(nj)], axis=1)
                    for h in range(halves):
                        cs = slice(h * MXU_WIDTH, (h + 1) * MXU_WIDTH)
                        ys_ref[rs, c * tn + h * MXU_WIDTH:c * tn + (h + 1) * MXU_WIDTH] = (
                            jnp.dot(a, wd[:, cs], preferred_element_type=F32) + bd_ref[:, cs])
                        if issue:
                            issue[h]()

                sub_tile(0, issue)
                for r in range(1, rows // tm):
                    pl.when(r * tm < nv)(functools.partial(sub_tile, r))


def _moe_call(block_expert, block_valid, table, hp, w_gate_up, b_gate_up3, w_down, b_down3, *, rows, tok_bits):
    t, d = hp.shape
    n_experts, _, two_ff = w_gate_up.shape
    d_ff = two_ff // 2
    n_blocks = table.shape[0] // rows
    n_real = t * TOP_K
    tf, tn, tm = 512, 512, 256
    nj, nd = d_ff // tf, d // tn
    table3 = table.reshape(n_blocks, 1, rows)

    def ja(g, s, nv):
        return jnp.where(nv[g] > 0, jnp.minimum(s, nj - 1), nj - 1)

    def jb(g, s, nv):
        return jnp.where(nv[g] > 0, jnp.maximum(s - nj, 0), nd - 1)

    tab = lambda fn: pl.BlockSpec((None, 1, rows), lambda g, s, be, nv: (fn(g), 0, 0), memory_space=pltpu.SMEM)
    grid_spec = pltpu.PrefetchScalarGridSpec(
        num_scalar_prefetch=2,
        grid=(n_blocks, nj + nd),
        in_specs=[
            tab(lambda g: 0),
            tab(lambda g: jnp.minimum(g + 1, n_blocks - 1)),
            tab(lambda g: jnp.maximum(g - 1, 0)),
            pl.BlockSpec(memory_space=pl.ANY),
            pl.BlockSpec((None, d, tf), lambda g, s, be, nv: (be[g], 0, ja(g, s, nv))),
            pl.BlockSpec((None, d, tf), lambda g, s, be, nv: (be[g], 0, nj + ja(g, s, nv))),
            pl.BlockSpec((None, 1, tf), lambda g, s, be, nv: (be[g], 0, ja(g, s, nv))),
            pl.BlockSpec((None, 1, tf), lambda g, s, be, nv: (be[g], 0, nj + ja(g, s, nv))),
            pl.BlockSpec((None, d_ff, tn), lambda g, s, be, nv: (be[g], 0, jb(g, s, nv))),
            pl.BlockSpec((None, 1, tn), lambda g, s, be, nv: (be[g], 0, jb(g, s, nv))),
        ],
        out_specs=pl.BlockSpec(memory_space=pl.ANY),
        scratch_shapes=[pltpu.VMEM((rows, d), F32),
                        pltpu.VMEM((rows, d), BF16),
                        pltpu.VMEM((nj, rows, tf), BF16),
                        pltpu.VMEM((rows, d), F32),
                        pltpu.SemaphoreType.DMA(()), pltpu.SemaphoreType.DMA(()), pltpu.SemaphoreType.DMA(())],
    )
    return pl.pallas_call(
        functools.partial(_moe_body, rows=rows, tm=tm, nj=nj, nd=nd, tn=tn, tok_bits=tok_bits, n_real=n_real),
        grid_spec=grid_spec,
        out_shape=jax.ShapeDtypeStruct((n_real + rows, d), F32),
        compiler_params=pltpu.CompilerParams(dimension_semantics=("arbitrary", "arbitrary"),
                                             vmem_limit_bytes=MOE_VMEM_LIMIT),
        name="moe",
    )(block_expert, block_valid, table3, table3, table3, hp, w_gate_up, w_gate_up, b_gate_up3, b_gate_up3,
      w_down, b_down3)


def _combine_body(yt_ref, x1_ref, rf_ref, mod_ref, l2g_ref, l2b_ref, o_ref, *, tc):
    rf = rf_ref[...]
    ffn = rf[:, 0:1] * yt_ref[0:tc, :]
    for k in range(1, TOP_K):
        ffn = ffn + rf[:, k:k + 1] * yt_ref[k * tc:(k + 1) * tc, :]
    g2 = mod_ref[5:6, :]
    o_ref[...] = _ln(DEEPNORM_ALPHA * x1_ref[...] + g2 * ffn) * l2g_ref[...] + l2b_ref[...]


def _combine_call(yt, x1, rf, mod3, l2g, l2b, *, seq):
    t, d = x1.shape
    tc = COMBINE_TILE
    tiles_per_batch = seq // tc
    return pl.pallas_call(
        functools.partial(_combine_body, tc=tc),
        grid=(t // tc,),
        in_specs=[pl.BlockSpec((TOP_K * tc, d), lambda i: (i, 0)),
                  pl.BlockSpec((tc, d), lambda i: (i, 0)),
                  pl.BlockSpec((tc, LANES), lambda i: (i, 0)),
                  pl.BlockSpec((None, 6, d), lambda i: (i // tiles_per_batch, 0, 0)),
                  pl.BlockSpec((1, d), lambda i: (0, 0)),
                  pl.BlockSpec((1, d), lambda i: (0, 0))],
        out_specs=pl.BlockSpec((tc, d), lambda i: (i, 0)),
        out_shape=jax.ShapeDtypeStruct((t, d), F32),
        compiler_params=_cparams(("arbitrary",)),
        name="combine",
    )(yt, x1, rf, mod3, l2g, l2b)


MOE_ROWS = 1024


def _layer(x, c, positions, ada_w, ada_b, w_in, w_spatial, b_spatial, gmlp_ln_g, gmlp_ln_b, w_out,
           ln1_g, ln1_b, router_w, router_b, w_gate_up, b_gate_up, w_down, b_down, ln2_g, ln2_b):
    batch, seq, d = x.shape
    t = batch * seq
    attn_w = ATTN_FRACTION_NUM * d // ATTN_FRACTION_DEN
    gmlp_w = d - attn_w
    n_heads = attn_w // HEAD_DIM
    n_experts = w_gate_up.shape[0]

    c_pad = jnp.pad(c, ((0, 8 - batch), (0, 0)))
    mod = _ada_call(c_pad, ada_w, ada_b.reshape(1, -1))[:batch]
    mod3 = mod.reshape(batch, 6, d)

    half = HEAD_DIM // 2
    inv_freq = jnp.power(ROPE_THETA, -jnp.arange(half, dtype=F32) * (2.0 / HEAD_DIM))
    invf = jnp.concatenate([inv_freq, inv_freq]).reshape(1, HEAD_DIM)
    x2 = x.reshape(t, d)
    qkv, uv = _inproj_call(x2, positions.reshape(t, 1), mod3, invf, w_in.astype(BF16),
                           seq=seq, attn_w=attn_w, gmlp_w=gmlp_w)
    attn = _attn_call(qkv, batch=batch, seq=seq, n_heads=n_heads)

    rw_pad = jnp.pad(router_w, ((0, 0), (0, LANES - n_experts)))
    rb_pad = jnp.pad(router_b.reshape(1, -1), ((0, 0), (0, LANES - n_experts)))
    x1, hp, ri, rf, cnt = _mixout_call(
        x2, attn, uv, mod3, w_out.astype(BF16), w_spatial, b_spatial.reshape(-1, 1),
        gmlp_ln_g.reshape(1, -1), gmlp_ln_b.reshape(1, -1), ln1_g.reshape(1, -1), ln1_b.reshape(1, -1),
        rw_pad, rb_pad, seq=seq, attn_w=attn_w, gmlp_w=gmlp_w)

    counts = cnt[0, :n_experts].astype(jnp.int32)
    padded = (counts + MOE_ROWS - 1) // MOE_ROWS * MOE_ROWS
    pad_end = jnp.cumsum(padded)
    pad_start = pad_end - padded
    pos = pad_start[ri[:, 0:TOP_K]] + ri[:, TOP_K:2 * TOP_K]
    n_blocks = t * TOP_K // MOE_ROWS + n_experts
    n_slots = n_blocks * MOE_ROWS
    block_row = jnp.arange(n_blocks, dtype=jnp.int32) * MOE_ROWS
    block_expert = jnp.minimum(jnp.sum(pad_end[None, :] <= block_row[:, None], axis=1), n_experts - 1).astype(jnp.int32)
    block_valid = jnp.clip(counts[block_expert] - (block_row - pad_start[block_expert]), 0, MOE_ROWS).astype(jnp.int32)
    n_used = pad_end[-1] // MOE_ROWS
    block_expert = jnp.where(block_valid > 0, block_expert, block_expert[n_used - 1])

    tok_bits = (t - 1).bit_length()
    n_real = t * TOP_K
    assert (n_real + MOE_ROWS) << tok_bits < 2 ** 31 and t % COMBINE_TILE == 0
    tok = jnp.arange(t, dtype=jnp.int32)[:, None]
    kk = jnp.arange(TOP_K, dtype=jnp.int32)[None, :]
    dest = (tok // COMBINE_TILE) * (TOP_K * COMBINE_TILE) + kk * COMBINE_TILE + tok % COMBINE_TILE
    entries = tok | (dest << tok_bits)
    init = (n_real + jnp.arange(n_slots, dtype=jnp.int32) % MOE_ROWS) << tok_bits
    table = _slotmap_call(pos.reshape(-1), entries.reshape(-1), init)

    yt = _moe_call(block_expert, block_valid, table, hp, w_gate_up, b_gate_up.reshape(n_experts, 1, -1),
                   w_down, b_down.reshape(n_experts, 1, -1), rows=MOE_ROWS, tok_bits=tok_bits)
    out = _combine_call(yt, x1, rf, mod3, ln2_g.reshape(1, -1), ln2_b.reshape(1, -1), seq=seq)
    return out.reshape(batch, seq, d)


def kernel(x, c, positions, ada_w, ada_b, w_in, w_spatial, b_spatial, gmlp_ln_g, gmlp_ln_b, w_out, ln1_g, ln1_b, router_w, router_b, w_gate_up, b_gate_up, w_down, b_down, ln2_g, ln2_b):
    assert ada_w.shape[0] == DEPTH
    params = (ada_w, ada_b, w_in, w_spatial, b_spatial, gmlp_ln_g, gmlp_ln_b, w_out, ln1_g, ln1_b,
              router_w, router_b, w_gate_up, b_gate_up, w_down, b_down, ln2_g, ln2_b)
    return _layer(x, c, positions, *[p.reshape(p.shape[1:]) for p in params])
```
